```python
import math
import jax
import jax.numpy as jnp
from jax import lax
import numpy as np

D_MODEL = 1024
BATCH = 2
SEQ = 16384
DEPTH = 2
DEC_BATCH = 8
DEC_SEQ = 8192
PAST_LEN = 128

N_MIXERS = 2
N_CONF_LAYERS = (DEPTH + 1) // 2
N_SC_LAYERS = DEPTH // 2
CONF_KERNEL = 31
SC_KERNEL = 3
N_GROUPS = 4
EXPERTS_PER_GROUP = 8
N_EXPERTS = N_GROUPS * EXPERTS_PER_GROUP
TOP_K = 2
D_EXPERT = 512
BLOCK = 256
RMS_EPS = 1e-6
LN_EPS = 1e-5

kernel_name = "hybrid_conformer_shortconv_hier_moe_encoder"


def rmsnorm(x, g):
    xf = x.astype(jnp.float32)
    y = xf * lax.rsqrt(jnp.mean(xf * xf, axis=-1, keepdims=True) + RMS_EPS)
    return (y * g.astype(jnp.float32)).astype(x.dtype)


def layernorm(x, g, b):
    xf = x.astype(jnp.float32)
    mu = jnp.mean(xf, axis=-1, keepdims=True)
    var = jnp.mean(jnp.square(xf - mu), axis=-1, keepdims=True)
    y = (xf - mu) * lax.rsqrt(var + LN_EPS)
    return (y * g.astype(jnp.float32) + b.astype(jnp.float32)).astype(x.dtype)


def depthwise_conv(x, w, pad):
    return lax.conv_general_dilated(
        x, w[:, None, :].astype(x.dtype), window_strides=(1,), padding=[(pad, pad)],
        dimension_numbers=("NWC", "WIO", "NWC"), feature_group_count=x.shape[-1])


def conformer_conv(h, w_pw1, b_pw1, w_dw, b_dw, ln_g, ln_b, w_pw2, b_pw2):
    a, g = jnp.split(h @ w_pw1 + b_pw1, 2, axis=-1)
    v = a * jax.nn.sigmoid(g)
    v = depthwise_conv(v, w_dw, CONF_KERNEL // 2) + b_dw
    v = jax.nn.silu(layernorm(v, ln_g, ln_b))
    return v @ w_pw2 + b_pw2


def short_gated_conv(h, w_in, w_dw, w_out):
    b, c, u = jnp.split(h @ w_in, 3, axis=-1)
    v = depthwise_conv(c * u, w_dw, SC_KERNEL // 2)
    return (b * v) @ w_out


def _round_up(n, m):
    return ((n + m - 1) // m) * m


def routed_experts(hf, expert_idx, gate, w_gate, w_up, w_down):
    T, D = hf.shape
    K = expert_idx.shape[1]
    N = T * K
    L = _round_up(N + N_EXPERTS * BLOCK, BLOCK)
    nb = L // BLOCK
    flat_e = expert_idx.reshape(N)
    flat_tok = jnp.repeat(jnp.arange(T, dtype=jnp.int32), K)
    flat_g = gate.reshape(N)
    order = jnp.argsort(flat_e)
    se, stok, sg = flat_e[order], flat_tok[order], flat_g[order]
    counts = jnp.bincount(flat_e, length=N_EXPERTS).astype(jnp.int32)
    padded = ((counts + BLOCK - 1) // BLOCK) * BLOCK
    pad_end = jnp.cumsum(padded)
    pad_start = pad_end - padded
    cnt_start = jnp.cumsum(counts) - counts
    dest = pad_start[se] + (jnp.arange(N, dtype=jnp.int32) - cnt_start[se])
    tok_buf = jnp.zeros((L,), jnp.int32).at[dest].set(stok)
    gate_buf = jnp.zeros((L,), hf.dtype).at[dest].set(sg.astype(hf.dtype))
    block_e = jnp.minimum(
        jnp.searchsorted(pad_end, jnp.arange(nb, dtype=jnp.int32) * BLOCK, side="right"),
        N_EXPERTS - 1).astype(jnp.int32)
    x_blocks = hf[tok_buf].reshape(nb, BLOCK, D)

    def expert_block(args):
        xb, e = args
        hidden = jax.nn.silu(xb @ w_gate[e]) * (xb @ w_up[e])
        return hidden @ w_down[e]

    y_blocks = lax.map(expert_block, (x_blocks, block_e))
    y = y_blocks.reshape(L, D) * gate_buf[:, None]
    return jnp.zeros_like(hf).at[tok_buf].add(y)


def hier_moe(h, w_rg, w_re, w_gate, w_up, w_down):
    Bsz, S, D = h.shape
    T = Bsz * S
    hf = h.reshape(T, D)
    h32 = hf.astype(jnp.float32)
    g_prob = jax.nn.softmax(h32 @ w_rg.astype(jnp.float32), axis=-1)
    g_top_p, g_top = lax.top_k(g_prob, 1)
    e_logits = (h32 @ w_re.astype(jnp.float32)).reshape(T, N_GROUPS, EXPERTS_PER_GROUP)
    e_logits = jnp.take_along_axis(e_logits, g_top[:, :, None], axis=1)[:, 0]
    e_prob = jax.nn.softmax(e_logits, axis=-1)
    e_top_p, e_top = lax.top_k(e_prob, TOP_K)
    e_top_p = e_top_p / jnp.sum(e_top_p, axis=-1, keepdims=True)
    gate = g_top_p * e_top_p
    expert_idx = (g_top * EXPERTS_PER_GROUP + e_top).astype(jnp.int32)
    y = routed_experts(hf, expert_idx, gate, w_gate, w_up, w_down)
    return y.reshape(Bsz, S, D)


def trunk(x, norm_mix, conf_w_pw1, conf_b_pw1, conf_w_dw, conf_b_dw, conf_ln_g, conf_ln_b,
          conf_w_pw2, conf_b_pw2, sc_w_in, sc_w_dw, sc_w_out, norm_ffn, router_group,
          router_expert, w_gate, w_up, w_down, final_norm):
    for i in range(DEPTH):
        h = rmsnorm(x, norm_mix[i])
        j = i // N_MIXERS
        if i % N_MIXERS == 0:
            m = conformer_conv(h, conf_w_pw1[j], conf_b_pw1[j], conf_w_dw[j], conf_b_dw[j],
                               conf_ln_g[j], conf_ln_b[j], conf_w_pw2[j], conf_b_pw2[j])
        else:
            m = short_gated_conv(h, sc_w_in[j], sc_w_dw[j], sc_w_out[j])
        x = x + m
        h = rmsnorm(x, norm_ffn[i])
        x = x + hier_moe(h, router_group[i], router_expert[i], w_gate[i], w_up[i], w_down[i])
    return rmsnorm(x, final_norm)


def setup_inputs(seed: int = 0) -> dict:
    key = jax.random.key(seed)
    ks = jax.random.split(key, 24)
    D, F = D_MODEL, D_EXPERT
    f32 = jnp.float32

    def nrm(k, shape, scale):
        return jax.random.normal(k, shape, f32) * scale

    def gain(k, shape):
        return 1.0 + 0.01 * jax.random.normal(k, shape, f32)

    return {
        "x_prompt": jax.random.normal(ks[0], (BATCH, SEQ, D), f32),
        "x_sample": jax.random.normal(ks[1], (DEC_BATCH, DEC_SEQ, D), f32),
        "norm_mix": gain(ks[2], (DEPTH, D)),
        "conf_w_pw1": nrm(ks[3], (N_CONF_LAYERS, D, 2 * D), D ** -0.5),
        "conf_b_pw1": nrm(ks[4], (N_CONF_LAYERS, 2 * D), 0.01),
        "conf_w_dw": nrm(ks[5], (N_CONF_LAYERS, CONF_KERNEL, D), CONF_KERNEL ** -0.5),
        "conf_b_dw": nrm(ks[6], (N_CONF_LAYERS, D), 0.01),
        "conf_ln_g": gain(ks[7], (N_CONF_LAYERS, D)),
        "conf_ln_b": nrm(ks[8], (N_CONF_LAYERS, D), 0.01),
        "conf_w_pw2": nrm(ks[9], (N_CONF_LAYERS, D, D), D ** -0.5),
        "conf_b_pw2": nrm(ks[10], (N_CONF_LAYERS, D), 0.01),
        "sc_w_in": nrm(ks[11], (N_SC_LAYERS, D, 3 * D), D ** -0.5),
        "sc_w_dw": nrm(ks[12], (N_SC_LAYERS, SC_KERNEL, D), SC_KERNEL ** -0.5),
        "sc_w_out": nrm(ks[13], (N_SC_LAYERS, D, D), D ** -0.5),
        "norm_ffn": gain(ks[14], (DEPTH, D)),
        "router_group": nrm(ks[15], (DEPTH, D, N_GROUPS), D ** -0.5),
        "router_expert": nrm(ks[16], (DEPTH, D, N_EXPERTS), D ** -0.5),
        "w_gate": nrm(ks[17], (DEPTH, N_EXPERTS, D, F), D ** -0.5),
        "w_up": nrm(ks[18], (DEPTH, N_EXPERTS, D, F), D ** -0.5),
        "w_down": nrm(ks[19], (DEPTH, N_EXPERTS, F, D), F ** -0.5),
        "final_norm": gain(ks[20], (D,)),
    }


def reference(x_prompt, x_sample, norm_mix, conf_w_pw1, conf_b_pw1, conf_w_dw, conf_b_dw,
              conf_ln_g, conf_ln_b, conf_w_pw2, conf_b_pw2, sc_w_in, sc_w_dw, sc_w_out,
              norm_ffn, router_group, router_expert, w_gate, w_up, w_down, final_norm):
    y_prompt = trunk(x_prompt, norm_mix, conf_w_pw1, conf_b_pw1, conf_w_dw, conf_b_dw,
                     conf_ln_g, conf_ln_b, conf_w_pw2, conf_b_pw2, sc_w_in, sc_w_dw, sc_w_out,
                     norm_ffn, router_group, router_expert, w_gate, w_up, w_down, final_norm)
    y_sample = trunk(x_sample, norm_mix, conf_w_pw1, conf_b_pw1, conf_w_dw, conf_b_dw,
                     conf_ln_g, conf_ln_b, conf_w_pw2, conf_b_pw2, sc_w_in, sc_w_dw, sc_w_out,
                     norm_ffn, router_group, router_expert, w_gate, w_up, w_down, final_norm)
    return (y_prompt, y_sample)
```

```python
import functools

import jax
import jax.numpy as jnp
from jax import lax
from jax.experimental import pallas as pl
from jax.experimental.pallas import tpu as pltpu

RMS_EPS = 1e-6
LN_EPS = 1e-5
N_GROUPS = 4
EXPERTS_PER_GROUP = 8
N_EXPERTS = N_GROUPS * EXPERTS_PER_GROUP
ROUTER_ROWS = 40
LANES = 128
TOKEN_TILE = 512
CONF_HALO = 16
SC_HALO = 8
MOE_BLOCK = 256
CONV_ROWS = 128
VMEM_LIMIT = 56 * 1024 * 1024

f32 = jnp.float32
bf16 = jnp.bfloat16


def _rms(x, g):
    ms = jnp.mean(x * x, axis=-1, keepdims=True)
    return x * lax.rsqrt(ms + RMS_EPS) * g


def _route(h, wra_ref, wrb_ref, eidx_ref, gate_ref):
    rows = h.shape[0]
    h_hi = h.astype(bf16)
    h_lo = (h - h_hi.astype(f32)).astype(bf16)
    dn = (((1,), (1,)), ((), ()))
    la = lax.dot_general(wra_ref[...], h_hi, dn, preferred_element_type=f32)
    lb = lax.dot_general(wrb_ref[...], h_lo, dn, preferred_element_type=f32)
    lt = la[:ROUTER_ROWS] + la[ROUTER_ROWS:] + lb
    el = lt[:N_EXPERTS]
    gl = lt[N_EXPERTS:N_EXPERTS + N_GROUPS]
    gm = jnp.max(gl, axis=0, keepdims=True)
    gp = jnp.exp(gl - gm)
    gprob = gp / jnp.sum(gp, axis=0, keepdims=True)
    g_top_p = jnp.max(gprob, axis=0, keepdims=True)
    iota_g = lax.broadcasted_iota(jnp.int32, (N_GROUPS, rows), 0)
    g_top = jnp.min(jnp.where(gprob == g_top_p, iota_g, N_GROUPS), axis=0, keepdims=True)
    elg = jnp.zeros((EXPERTS_PER_GROUP, rows), f32)
    for g in range(N_GROUPS):
        sl = el[g * EXPERTS_PER_GROUP:(g + 1) * EXPERTS_PER_GROUP]
        elg = elg + jnp.where(g_top == g, sl, 0.0)
    em = jnp.max(elg, axis=0, keepdims=True)
    ep = jnp.exp(elg - em)
    eprob = ep / jnp.sum(ep, axis=0, keepdims=True)
    iota_e = lax.broadcasted_iota(jnp.int32, (EXPERTS_PER_GROUP, rows), 0)
    p1 = jnp.max(eprob, axis=0, keepdims=True)
    i1 = jnp.min(jnp.where(eprob == p1, iota_e, EXPERTS_PER_GROUP), axis=0, keepdims=True)
    rest = jnp.where(iota_e == i1, -1.0, eprob)
    p2 = jnp.max(rest, axis=0, keepdims=True)
    i2 = jnp.min(jnp.where(rest == p2, iota_e, EXPERTS_PER_GROUP), axis=0, keepdims=True)
    den = p1 + p2
    eidx_ref[0:1, :] = g_top * EXPERTS_PER_GROUP + i1
    eidx_ref[1:2, :] = g_top * EXPERTS_PER_GROUP + i2
    gate_ref[0:1, :] = g_top_p * (p1 / den)
    gate_ref[1:2, :] = g_top_p * (p2 / den)


def _seq_edges(i, n_tiles_a, tiles_per_seq_a, tiles_per_seq_b):
    in_a = i < n_tiles_a
    per = jnp.where(in_a, tiles_per_seq_a, tiles_per_seq_b)
    j = jnp.where(in_a, i, i - n_tiles_a) % per
    return j == 0, j == per - 1


def _conf_body(xprev_ref, xmain_ref, xnext_ref, nmix_ref, w1_ref, b1_ref, wdw_ref, bdw_ref,
               lng_ref, lnb_ref, w2_ref, b2_ref, nffn_ref, wra_ref, wrb_ref,
               x1_ref, h_ref, eidx_ref, gate_ref, xn_scr, v_scr, u_scr,
               *, n_tiles_a, tiles_per_seq_a, tiles_per_seq_b):
    ts, d = xmain_ref.shape
    halo = CONF_HALO
    w = ts + 2 * halo
    n_col = d // LANES
    taps = wdw_ref.shape[0]
    i = pl.program_id(0)
    is_start, is_end = _seq_edges(i, n_tiles_a, tiles_per_seq_a, tiles_per_seq_b)

    g = nmix_ref[...]
    xn_scr[0:halo, :] = _rms(xprev_ref[...], g).astype(bf16)
    xn_scr[halo:halo + ts, :] = _rms(xmain_ref[...], g).astype(bf16)
    xn_scr[halo + ts:w, :] = _rms(xnext_ref[...], g).astype(bf16)
    z = jnp.dot(xn_scr[...], w1_ref[...], preferred_element_type=f32) + b1_ref[...]
    v = z[:, :d] * jax.nn.sigmoid(z[:, d:])
    rows = lax.broadcasted_iota(jnp.int32, (w, 1), 0)
    outside = (is_start & (rows < halo)) | (is_end & (rows >= halo + ts))
    v = jnp.where(outside, 0.0, v)
    for j in range(n_col):
        v_scr[j] = v[:, j * LANES:(j + 1) * LANES]

    off = halo - taps // 2
    for j in range(n_col):
        cs = slice(j * LANES, (j + 1) * LANES)
        wj = wdw_ref[:, cs]
        wb = [jnp.broadcast_to(wj[k:k + 1, :], (CONV_ROWS, LANES)) for k in range(taps)]
        bj = bdw_ref[:, cs]
        for c in range(ts // CONV_ROWS):
            acc = v_scr[j, pl.ds(c * CONV_ROWS + off, CONV_ROWS), :] * wb[0]
            for k in range(1, taps):
                acc = acc + v_scr[j, pl.ds(c * CONV_ROWS + off + k, CONV_ROWS), :] * wb[k]
            u_scr[pl.ds(c * CONV_ROWS, CONV_ROWS), cs] = acc + bj

    u = u_scr[...]
    mu = jnp.mean(u, axis=-1, keepdims=True)
    uc = u - mu
    var = jnp.mean(uc * uc, axis=-1, keepdims=True)
    y = uc * lax.rsqrt(var + LN_EPS) * lng_ref[...] + lnb_ref[...]
    s = y * jax.nn.sigmoid(y)
    m = jnp.dot(s.astype(bf16), w2_ref[...], preferred_element_type=f32) + b2_ref[...]
    x1 = xmain_ref[...] + m
    x1_ref[...] = x1
    h = _rms(x1, nffn_ref[...])
    h_ref[...] = h
    _route(h, wra_ref, wrb_ref, eidx_ref, gate_ref)


def _const_spec(shape):
    nd = len(shape)
    return pl.BlockSpec(shape, lambda i: (0,) * nd, pipeline_mode=pl.Buffered(1))


def _conf_layer(x, seq_layout, nmix, w1, b1, wdw, bdw, lng, lnb, w2, b2, nffn, wra, wrb):
    t, d = x.shape
    ts = TOKEN_TILE
    n_tiles = t // ts
    n_tiles_a, tiles_per_seq_a, tiles_per_seq_b = seq_layout
    hb = ts // CONF_HALO
    n_hblk = t // CONF_HALO
    in_specs = [
        pl.BlockSpec((CONF_HALO, d), lambda i: (jnp.maximum(i * hb - 1, 0), 0)),
        pl.BlockSpec((ts, d), lambda i: (i, 0)),
        pl.BlockSpec((CONF_HALO, d), lambda i: (jnp.minimum((i + 1) * hb, n_hblk - 1), 0)),
    ] + [_const_spec(a.shape) for a in (nmix, w1, b1, wdw, bdw, lng, lnb, w2, b2, nffn, wra, wrb)]
    out_specs = [
        pl.BlockSpec((ts, d), lambda i: (i, 0)),
        pl.BlockSpec((ts, d), lambda i: (i, 0)),
        pl.BlockSpec((2, ts), lambda i: (0, i)),
        pl.BlockSpec((2, ts), lambda i: (0, i)),
    ]
    out_shape = [
        jax.ShapeDtypeStruct((t, d), f32),
        jax.ShapeDtypeStruct((t, d), f32),
        jax.ShapeDtypeStruct((2, t), jnp.int32),
        jax.ShapeDtypeStruct((2, t), f32),
    ]
    w = ts + 2 * CONF_HALO
    body = functools.partial(_conf_body, n_tiles_a=n_tiles_a, tiles_per_seq_a=tiles_per_seq_a,
                             tiles_per_seq_b=tiles_per_seq_b)
    return pl.pallas_call(
        body,
        grid=(n_tiles,),
        in_specs=in_specs,
        out_specs=out_specs,
        out_shape=out_shape,
        scratch_shapes=[
            pltpu.VMEM((w, d), bf16),
            pltpu.VMEM((d // LANES, w, LANES), f32),
            pltpu.VMEM((ts, d), f32),
        ],
        compiler_params=pltpu.CompilerParams(
            dimension_semantics=("arbitrary",), vmem_limit_bytes=VMEM_LIMIT),
        name="conf_layer",
    )(x, x, x, nmix, w1, b1, wdw, bdw, lng, lnb, w2, b2, nffn, wra, wrb)


def _route_plan(eidx, gates):
    t = eidx.shape[1]
    n = 2 * t
    bm = MOE_BLOCK
    length = ((n + N_EXPERTS * bm + bm - 1) // bm) * bm
    nb = length // bm
    e_flat = eidx.reshape(n)
    tok_flat = jnp.tile(jnp.arange(t, dtype=jnp.int32), 2)
    onehot = (e_flat[:, None] == jnp.arange(N_EXPERTS, dtype=jnp.int32)[None, :]).astype(jnp.int32)
    csum = jnp.cumsum(onehot, axis=0)
    rank = jnp.take_along_axis(csum, e_flat[:, None], axis=1)[:, 0] - 1
    counts = csum[-1]
    padded = ((counts + bm - 1) // bm) * bm
    pad_end = jnp.cumsum(padded)
    pad_start = pad_end - padded
    dest = pad_start[e_flat] + rank
    tok_buf = jnp.zeros((length,), jnp.int32).at[dest].set(tok_flat)
    gate_buf = jnp.zeros((length,), f32).at[dest].set(gates.reshape(n))
    block_e = jnp.minimum(
        jnp.searchsorted(pad_end, jnp.arange(nb, dtype=jnp.int32) * bm, side="right"),
        N_EXPERTS - 1).astype(jnp.int32)
    n_used = (pad_end[-1] // bm).astype(jnp.int32).reshape(1)
    return dict(tok=tok_buf.reshape(nb, 1, bm), gate=gate_buf.reshape(length, 1),
                block_e=block_e, n_used=n_used, pos=dest.reshape(2, t).astype(jnp.int32))


def _window_positions(pos, ts, halo):
    t = pos.shape[1]
    n_tiles = t // ts
    w = ts + 2 * halo
    tok = jnp.arange(n_tiles, dtype=jnp.int32)[:, None] * ts - halo + jnp.arange(w, dtype=jnp.int32)[None, :]
    tok = jnp.clip(tok, 0, t - 1)
    return jnp.concatenate([pos[0][tok], pos[1][tok]], axis=1).reshape(n_tiles, 1, 2 * w)


def _issue_row_gather(idx_ref, n_rows, src_hbm, dst_of_row, sem):
    unroll = 8

    def body(r8, carry):
        for u in range(unroll):
            r = r8 * unroll + u
            tok = idx_ref[0, 0, r]
            pltpu.make_async_copy(src_hbm.at[pl.ds(tok, 1), :], dst_of_row(r), sem).start()
        return carry

    lax.fori_loop(0, n_rows // unroll, body, 0)


def _moe_body(be_ref, nused_ref, tok0_ref, tokn_ref, h_hbm, gate_ref, wgu_ref, wd_ref,
              y_ref, buf, sem):
    bm = buf.shape[1]
    d_exp = wd_ref.shape[1]
    b = pl.program_id(0)
    n_used = nused_ref[0]

    def issue(idx_ref, slot):
        _issue_row_gather(idx_ref, bm, h_hbm, lambda r: buf.at[slot, pl.ds(r, 1), :], sem.at[slot])

    @pl.when(b == 0)
    def _():
        issue(tok0_ref, 0)

    @pl.when(b + 1 < n_used)
    def _():
        issue(tokn_ref, (b + 1) % 2)

    @pl.when(b < n_used)
    def _():
        slot = b % 2
        pltpu.make_async_copy(h_hbm.at[pl.ds(0, bm), :], buf.at[slot], sem.at[slot]).wait()
        x = buf[slot].astype(bf16)
        gu = jnp.dot(x, wgu_ref[0], preferred_element_type=f32)
        gt = gu[:, :d_exp]
        hid = gt * jax.nn.sigmoid(gt) * gu[:, d_exp:]
        y = jnp.dot(hid.astype(bf16), wd_ref[0], preferred_element_type=f32)
        y_ref[...] = y * gate_ref[...]

    @pl.when(b >= n_used)
    def _():
        y_ref[...] = jnp.zeros_like(y_ref)


def _moe(h, plan, wgu, wd):
    t, d = h.shape
    nb, _, bm = plan["tok"].shape
    length = nb * bm
    n_exp, _, d_gu = wgu.shape
    d_exp = wd.shape[1]
    grid_spec = pltpu.PrefetchScalarGridSpec(
        num_scalar_prefetch=2,
        grid=(nb,),
        in_specs=[
            pl.BlockSpec((1, 1, bm), lambda b, be, nu: (0, 0, 0), memory_space=pltpu.SMEM),
            pl.BlockSpec((1, 1, bm), lambda b, be, nu: (jnp.minimum(b + 1, nb - 1), 0, 0),
                         memory_space=pltpu.SMEM),
            pl.BlockSpec(memory_space=pl.ANY),
            pl.BlockSpec((bm, 1), lambda b, be, nu: (b, 0)),
            pl.BlockSpec((1, d, d_gu), lambda b, be, nu: (be[b], 0, 0)),
            pl.BlockSpec((1, d_exp, d), lambda b, be, nu: (be[b], 0, 0)),
        ],
        out_specs=pl.BlockSpec((bm, d), lambda b, be, nu: (b, 0)),
        scratch_shapes=[pltpu.VMEM((2, bm, d), f32), pltpu.SemaphoreType.DMA((2,))],
    )
    return pl.pallas_call(
        _moe_body,
        grid_spec=grid_spec,
        out_shape=jax.ShapeDtypeStruct((length, d), f32),
        compiler_params=pltpu.CompilerParams(
            dimension_semantics=("arbitrary",), vmem_limit_bytes=VMEM_LIMIT),
        name="moe",
    )(plan["block_e"], plan["n_used"], plan["tok"], plan["tok"], h, plan["gate"], wgu, wd)


def _combine_gather(i, n_tiles, pos0_ref, posn_ref, y_hbm, gbuf, sem):
    n_rows = gbuf.shape[1]

    def issue(idx_ref, slot):
        _issue_row_gather(idx_ref, n_rows, y_hbm, lambda r: gbuf.at[slot, pl.ds(r, 1), :],
                          sem.at[slot])

    @pl.when(i == 0)
    def _():
        issue(pos0_ref, 0)

    @pl.when(i + 1 < n_tiles)
    def _():
        issue(posn_ref, (i + 1) % 2)

    slot = i % 2
    pltpu.make_async_copy(y_hbm.at[pl.ds(0, n_rows), :], gbuf.at[slot], sem.at[slot]).wait()
    return slot


def _sc_body(pos0_ref, posn_ref, xprev_ref, xmain_ref, xnext_ref, y_hbm, nmix_ref, win_ref,
             wdw_ref, wout_ref, nffn_ref, wra_ref, wrb_ref,
             x2_ref, h_ref, eidx_ref, gate_ref, gbuf, sem, xin_scr, cu_scr,
             *, n_tiles, n_tiles_a, tiles_per_seq_a, tiles_per_seq_b):
    ts, d = xmain_ref.shape
    halo = SC_HALO
    w = ts + 2 * halo
    n_col = d // LANES
    i = pl.program_id(0)
    is_start, is_end = _seq_edges(i, n_tiles_a, tiles_per_seq_a, tiles_per_seq_b)
    slot = _combine_gather(i, n_tiles, pos0_ref, posn_ref, y_hbm, gbuf, sem)

    xin_scr[0:halo, :] = xprev_ref[...]
    xin_scr[halo:halo + ts, :] = xmain_ref[...]
    xin_scr[halo + ts:w, :] = xnext_ref[...]
    xc = xin_scr[...] + gbuf[slot, 0:w, :] + gbuf[slot, w:2 * w, :]
    hn = _rms(xc, nmix_ref[...]).astype(bf16)
    z = jnp.dot(hn, win_ref[...], preferred_element_type=f32)
    cu = z[:, d:2 * d] * z[:, 2 * d:]
    rows = lax.broadcasted_iota(jnp.int32, (w, 1), 0)
    outside = (is_start & (rows < halo)) | (is_end & (rows >= halo + ts))
    cu = jnp.where(outside, 0.0, cu)
    for j in range(n_col):
        cu_scr[j] = cu[:, j * LANES:(j + 1) * LANES]
    taps = wdw_ref.shape[0]
    off = halo - taps // 2
    parts = []
    for j in range(n_col):
        cs = slice(j * LANES, (j + 1) * LANES)
        wj = wdw_ref[:, cs]
        acc = cu_scr[j, pl.ds(off, ts), :] * wj[0:1, :]
        for k in range(1, taps):
            acc = acc + cu_scr[j, pl.ds(off + k, ts), :] * wj[k:k + 1, :]
        parts.append(acc)
    conv = jnp.concatenate(parts, axis=1)
    bv = z[halo:halo + ts, :d] * conv
    m = jnp.dot(bv.astype(bf16), wout_ref[...], preferred_element_type=f32)
    x2 = xc[halo:halo + ts, :] + m
    x2_ref[...] = x2
    h = _rms(x2, nffn_ref[...])
    h_ref[...] = h
    _route(h, wra_ref, wrb_ref, eidx_ref, gate_ref)


def _sc_layer(x1, y, posw, seq_layout, nmix, win, wdw, wout, nffn, wra, wrb):
    t, d = x1.shape
    ts = TOKEN_TILE
    n_tiles = t // ts
    n_tiles_a, tiles_per_seq_a, tiles_per_seq_b = seq_layout
    hb = ts // SC_HALO
    n_hblk = t // SC_HALO
    w = ts + 2 * SC_HALO
    smem = pltpu.SMEM
    in_specs = [
        pl.BlockSpec((1, 1, 2 * w), lambda i: (0, 0, 0), memory_space=smem),
        pl.BlockSpec((1, 1, 2 * w), lambda i: (jnp.minimum(i + 1, n_tiles - 1), 0, 0), memory_space=smem),
        pl.BlockSpec((SC_HALO, d), lambda i: (jnp.maximum(i * hb - 1, 0), 0)),
        pl.BlockSpec((ts, d), lambda i: (i, 0)),
        pl.BlockSpec((SC_HALO, d), lambda i: (jnp.minimum((i + 1) * hb, n_hblk - 1), 0)),
        pl.BlockSpec(memory_space=pl.ANY),
    ] + [_const_spec(a.shape) for a in (nmix, win, wdw, wout, nffn, wra, wrb)]
    out_specs = [
        pl.BlockSpec((ts, d), lambda i: (i, 0)),
        pl.BlockSpec((ts, d), lambda i: (i, 0)),
        pl.BlockSpec((2, ts), lambda i: (0, i)),
        pl.BlockSpec((2, ts), lambda i: (0, i)),
    ]
    out_shape = [
        jax.ShapeDtypeStruct((t, d), f32),
        jax.ShapeDtypeStruct((t, d), f32),
        jax.ShapeDtypeStruct((2, t), jnp.int32),
        jax.ShapeDtypeStruct((2, t), f32),
    ]
    body = functools.partial(_sc_body, n_tiles=n_tiles, n_tiles_a=n_tiles_a,
                             tiles_per_seq_a=tiles_per_seq_a, tiles_per_seq_b=tiles_per_seq_b)
    return pl.pallas_call(
        body,
        grid=(n_tiles,),
        in_specs=in_specs,
        out_specs=out_specs,
        out_shape=out_shape,
        scratch_shapes=[
            pltpu.VMEM((2, 2 * w, d), f32),
            pltpu.SemaphoreType.DMA((2,)),
            pltpu.VMEM((w, d), f32),
            pltpu.VMEM((d // LANES, w, LANES), f32),
        ],
        compiler_params=pltpu.CompilerParams(
            dimension_semantics=("arbitrary",), vmem_limit_bytes=VMEM_LIMIT),
        name="sc_layer",
    )(posw, posw, x1, x1, x1, y, nmix, win, wdw, wout, nffn, wra, wrb)


def _final_body(pos0_ref, posn_ref, x_ref, y_hbm, g_ref, o_ref, gbuf, sem, *, n_tiles):
    ts = x_ref.shape[0]
    i = pl.program_id(0)
    slot = _combine_gather(i, n_tiles, pos0_ref, posn_ref, y_hbm, gbuf, sem)
    xc = x_ref[...] + gbuf[slot, 0:ts, :] + gbuf[slot, ts:2 * ts, :]
    o_ref[...] = _rms(xc, g_ref[...])


def _final(x2, y, posw, g, tile_off, n_tiles):
    t, d = x2.shape
    ts = TOKEN_TILE
    smem = pltpu.SMEM
    in_specs = [
        pl.BlockSpec((1, 1, 2 * ts), lambda i: (tile_off, 0, 0), memory_space=smem),
        pl.BlockSpec((1, 1, 2 * ts), lambda i: (tile_off + jnp.minimum(i + 1, n_tiles - 1), 0, 0),
                     memory_space=smem),
        pl.BlockSpec((ts, d), lambda i: (tile_off + i, 0)),
        pl.BlockSpec(memory_space=pl.ANY),
        _const_spec(g.shape),
    ]
    return pl.pallas_call(
        functools.partial(_final_body, n_tiles=n_tiles),
        grid=(n_tiles,),
        in_specs=in_specs,
        out_specs=pl.BlockSpec((ts, d), lambda i: (i, 0)),
        out_shape=jax.ShapeDtypeStruct((n_tiles * ts, d), f32),
        scratch_shapes=[pltpu.VMEM((2, 2 * ts, d), f32), pltpu.SemaphoreType.DMA((2,))],
        compiler_params=pltpu.CompilerParams(
            dimension_semantics=("arbitrary",), vmem_limit_bytes=VMEM_LIMIT),
        name="final_norm",
    )(posw, posw, x2, y, g)


def _router_weights(w_rg, w_re):
    d = w_rg.shape[0]
    wt = jnp.concatenate([w_re.T, w_rg.T, jnp.zeros((ROUTER_ROWS - N_EXPERTS - N_GROUPS, d), f32)], axis=0)
    hi = wt.astype(bf16)
    lo = (wt - hi.astype(f32)).astype(bf16)
    return jnp.concatenate([hi, lo], axis=0), hi


def kernel(x_prompt, x_sample, norm_mix, conf_w_pw1, conf_b_pw1, conf_w_dw, conf_b_dw, conf_ln_g,
           conf_ln_b, conf_w_pw2, conf_b_pw2, sc_w_in, sc_w_dw, sc_w_out, norm_ffn, router_group,
           router_expert, w_gate, w_up, w_down, final_norm):
    bp, sp, d = x_prompt.shape
    bs, ss, _ = x_sample.shape
    ts = TOKEN_TILE
    assert norm_mix.shape[0] == 2 and sp % ts == 0 and ss % ts == 0 and d % LANES == 0
    tp, tsamp = bp * sp, bs * ss
    seq_layout = (tp // ts, sp // ts, ss // ts)
    x = jnp.concatenate([x_prompt.reshape(tp, d), x_sample.reshape(tsamp, d)], axis=0)

    row = lambda a: a.reshape(1, -1)
    wgu = jnp.concatenate([w_gate, w_up], axis=-1).astype(bf16)
    wd = w_down.astype(bf16)
    wra0, wrb0 = _router_weights(router_group[0], router_expert[0])
    wra1, wrb1 = _router_weights(router_group[1], router_expert[1])

    x1, h1, eidx1, gates1 = _conf_layer(
        x, seq_layout, row(norm_mix[0]), conf_w_pw1[0].astype(bf16), row(conf_b_pw1[0]),
        conf_w_dw[0], row(conf_b_dw[0]), row(conf_ln_g[0]), row(conf_ln_b[0]),
        conf_w_pw2[0].astype(bf16), row(conf_b_pw2[0]), row(norm_ffn[0]), wra0, wrb0)
    plan1 = _route_plan(eidx1, gates1)
    y1 = _moe(h1, plan1, wgu[0], wd[0])

    posw1 = _window_positions(plan1["pos"], ts, SC_HALO)
    x2, h2, eidx2, gates2 = _sc_layer(
        x1, y1, posw1, seq_layout, row(norm_mix[1]), sc_w_in[0].astype(bf16), sc_w_dw[0],
        sc_w_out[0].astype(bf16), row(norm_ffn[1]), wra1, wrb1)
    plan2 = _route_plan(eidx2, gates2)
    y2 = _moe(h2, plan2, wgu[1], wd[1])

    posw2 = _window_positions(plan2["pos"], ts, 0)
    fg = row(final_norm)
    out_p = _final(x2, y2, posw2, fg, 0, tp // ts)
    out_s = _final(x2, y2, posw2, fg, tp // ts, tsamp // ts)
    return out_p.reshape(bp, sp, d), out_s.reshape(bs, ss, d)
```

```python
import functools

import jax
import jax.numpy as jnp
from jax import lax
from jax.experimental import pallas as pl
from jax.experimental.pallas import tpu as pltpu

RMS_EPS = 1e-6
LN_EPS = 1e-5
N_GROUPS = 4
EXPERTS_PER_GROUP = 8
N_EXPERTS = N_GROUPS * EXPERTS_PER_GROUP
ROUTER_ROWS = 40
LANES = 128
SUBLANES = 8
TOKEN_TILE = 512
CONF_HALO = 16
SC_HALO = 8
MOE_BLOCK = 256
CONV_ROWS = 128
VMEM_LIMIT = 56 * 1024 * 1024

f32 = jnp.float32
bf16 = jnp.bfloat16


def _rms(x, g):
    ms = jnp.mean(x * x, axis=-1, keepdims=True)
    return x * lax.rsqrt(ms + RMS_EPS) * g


def _route(h, wra_ref, wrb_ref, tri_ref, eidx_ref, gate_ref, lrank_ref):
    rows = h.shape[0]
    h_hi = h.astype(bf16)
    h_lo = (h - h_hi.astype(f32)).astype(bf16)
    dn = (((1,), (1,)), ((), ()))
    la = lax.dot_general(wra_ref[...], h_hi, dn, preferred_element_type=f32)
    lb = lax.dot_general(wrb_ref[...], h_lo, dn, preferred_element_type=f32)
    lt = la[:ROUTER_ROWS] + la[ROUTER_ROWS:] + lb
    el = lt[:N_EXPERTS]
    gl = lt[N_EXPERTS:N_EXPERTS + N_GROUPS]
    gm = jnp.max(gl, axis=0, keepdims=True)
    gp = jnp.exp(gl - gm)
    gprob = gp / jnp.sum(gp, axis=0, keepdims=True)
    g_top_p = jnp.max(gprob, axis=0, keepdims=True)
    iota_g = lax.broadcasted_iota(jnp.int32, (N_GROUPS, rows), 0)
    g_top = jnp.min(jnp.where(gprob == g_top_p, iota_g, N_GROUPS), axis=0, keepdims=True)
    elg = jnp.zeros((EXPERTS_PER_GROUP, rows), f32)
    for g in range(N_GROUPS):
        sl = el[g * EXPERTS_PER_GROUP:(g + 1) * EXPERTS_PER_GROUP]
        elg = elg + jnp.where(g_top == g, sl, 0.0)
    em = jnp.max(elg, axis=0, keepdims=True)
    ep = jnp.exp(elg - em)
    eprob = ep / jnp.sum(ep, axis=0, keepdims=True)
    iota_e = lax.broadcasted_iota(jnp.int32, (EXPERTS_PER_GROUP, rows), 0)
    p1 = jnp.max(eprob, axis=0, keepdims=True)
    i1 = jnp.min(jnp.where(eprob == p1, iota_e, EXPERTS_PER_GROUP), axis=0, keepdims=True)
    rest = jnp.where(iota_e == i1, -1.0, eprob)
    p2 = jnp.max(rest, axis=0, keepdims=True)
    i2 = jnp.min(jnp.where(rest == p2, iota_e, EXPERTS_PER_GROUP), axis=0, keepdims=True)
    den = p1 + p2
    e1 = g_top * EXPERTS_PER_GROUP + i1
    e2 = g_top * EXPERTS_PER_GROUP + i2
    eidx_ref[0:1, :] = e1
    eidx_ref[1:2, :] = e2
    gate_ref[0:1, :] = g_top_p * (p1 / den)
    gate_ref[1:2, :] = g_top_p * (p2 / den)

    iota_x = lax.broadcasted_iota(jnp.int32, (N_EXPERTS, rows), 0)
    oh = jnp.concatenate([iota_x == e1, iota_x == e2], axis=0).astype(f32)
    before = jnp.dot(oh.astype(bf16), tri_ref[...], preferred_element_type=f32)
    cnt1 = jnp.sum(oh[:N_EXPERTS], axis=1, keepdims=True)
    r1 = jnp.sum(oh[:N_EXPERTS] * before[:N_EXPERTS], axis=0, keepdims=True)
    r2 = jnp.sum(oh[N_EXPERTS:] * (before[N_EXPERTS:] + cnt1), axis=0, keepdims=True)
    lrank_ref[0:1, :] = r1.astype(jnp.int32)
    lrank_ref[1:2, :] = r2.astype(jnp.int32)


def _seq_edges(i, n_tiles_a, tiles_per_seq_a, tiles_per_seq_b):
    in_a = i < n_tiles_a
    per = jnp.where(in_a, tiles_per_seq_a, tiles_per_seq_b)
    j = jnp.where(in_a, i, i - n_tiles_a) % per
    return j == 0, j == per - 1


def _const_spec(shape):
    nd = len(shape)
    return pl.BlockSpec(shape, lambda i: (0,) * nd, pipeline_mode=pl.Buffered(1))


def _layer_out(t, d, ts):
    out_specs = [
        pl.BlockSpec((ts, d), lambda i: (i, 0)),
        pl.BlockSpec((ts, d), lambda i: (i, 0)),
        pl.BlockSpec((2, ts), lambda i: (0, i)),
        pl.BlockSpec((2, ts), lambda i: (0, i)),
        pl.BlockSpec((2, ts), lambda i: (0, i)),
    ]
    out_shape = [
        jax.ShapeDtypeStruct((t, d), f32),
        jax.ShapeDtypeStruct((t, d), f32),
        jax.ShapeDtypeStruct((2, t), jnp.int32),
        jax.ShapeDtypeStruct((2, t), f32),
        jax.ShapeDtypeStruct((2, t), jnp.int32),
    ]
    return out_specs, out_shape


def _conf_body(xprev_ref, xmain_ref, xnext_ref, nmix_ref, w1_ref, b1_ref, wdw_ref, bdw_ref,
               lng_ref, lnb_ref, w2_ref, b2_ref, nffn_ref, wra_ref, wrb_ref, tri_ref,
               x1_ref, h_ref, eidx_ref, gate_ref, lrank_ref, xn_scr, v_scr, u_scr,
               *, n_tiles_a, tiles_per_seq_a, tiles_per_seq_b):
    ts, d = xmain_ref.shape
    halo = CONF_HALO
    w = ts + 2 * halo
    n_col = d // LANES
    taps = wdw_ref.shape[0]
    i = pl.program_id(0)
    is_start, is_end = _seq_edges(i, n_tiles_a, tiles_per_seq_a, tiles_per_seq_b)

    g = nmix_ref[...]
    xn_scr[0:halo, :] = _rms(xprev_ref[...], g).astype(bf16)
    xn_scr[halo:halo + ts, :] = _rms(xmain_ref[...], g).astype(bf16)
    xn_scr[halo + ts:w, :] = _rms(xnext_ref[...], g).astype(bf16)
    z = jnp.dot(xn_scr[...], w1_ref[...], preferred_element_type=f32) + b1_ref[...]
    v = z[:, :d] * jax.nn.sigmoid(z[:, d:])
    rows = lax.broadcasted_iota(jnp.int32, (w, 1), 0)
    outside = (is_start & (rows < halo)) | (is_end & (rows >= halo + ts))
    v = jnp.where(outside, 0.0, v)
    for j in range(n_col):
        v_scr[j] = v[:, j * LANES:(j + 1) * LANES]

    off = halo - taps // 2
    for j in range(n_col):
        cs = slice(j * LANES, (j + 1) * LANES)
        wj = wdw_ref[:, cs]
        wb = [jnp.broadcast_to(wj[k:k + 1, :], (CONV_ROWS, LANES)) for k in range(taps)]
        bj = bdw_ref[:, cs]
        for c in range(ts // CONV_ROWS):
            acc = v_scr[j, pl.ds(c * CONV_ROWS + off, CONV_ROWS), :] * wb[0]
            for k in range(1, taps):
                acc = acc + v_scr[j, pl.ds(c * CONV_ROWS + off + k, CONV_ROWS), :] * wb[k]
            u_scr[pl.ds(c * CONV_ROWS, CONV_ROWS), cs] = acc + bj

    u = u_scr[...]
    mu = jnp.mean(u, axis=-1, keepdims=True)
    uc = u - mu
    var = jnp.mean(uc * uc, axis=-1, keepdims=True)
    y = uc * lax.rsqrt(var + LN_EPS) * lng_ref[...] + lnb_ref[...]
    s = y * jax.nn.sigmoid(y)
    m = jnp.dot(s.astype(bf16), w2_ref[...], preferred_element_type=f32) + b2_ref[...]
    x1 = xmain_ref[...] + m
    x1_ref[...] = x1
    h = _rms(x1, nffn_ref[...])
    h_ref[...] = h
    _route(h, wra_ref, wrb_ref, tri_ref, eidx_ref, gate_ref, lrank_ref)


def _conf_layer(x, seq_layout, nmix, w1, b1, wdw, bdw, lng, lnb, w2, b2, nffn, wra, wrb, tri):
    t, d = x.shape
    ts = TOKEN_TILE
    n_tiles = t // ts
    n_tiles_a, tiles_per_seq_a, tiles_per_seq_b = seq_layout
    hb = ts // CONF_HALO
    n_hblk = t // CONF_HALO
    consts = (nmix, w1, b1, wdw, bdw, lng, lnb, w2, b2, nffn, wra, wrb, tri)
    in_specs = [
        pl.BlockSpec((CONF_HALO, d), lambda i: (jnp.maximum(i * hb - 1, 0), 0)),
        pl.BlockSpec((ts, d), lambda i: (i, 0)),
        pl.BlockSpec((CONF_HALO, d), lambda i: (jnp.minimum((i + 1) * hb, n_hblk - 1), 0)),
    ] + [_const_spec(a.shape) for a in consts]
    out_specs, out_shape = _layer_out(t, d, ts)
    w = ts + 2 * CONF_HALO
    body = functools.partial(_conf_body, n_tiles_a=n_tiles_a, tiles_per_seq_a=tiles_per_seq_a,
                             tiles_per_seq_b=tiles_per_seq_b)
    return pl.pallas_call(
        body,
        grid=(n_tiles,),
        in_specs=in_specs,
        out_specs=out_specs,
        out_shape=out_shape,
        scratch_shapes=[
            pltpu.VMEM((w, d), bf16),
            pltpu.VMEM((d // LANES, w, LANES), f32),
            pltpu.VMEM((ts, d), f32),
        ],
        compiler_params=pltpu.CompilerParams(
            dimension_semantics=("arbitrary",), vmem_limit_bytes=VMEM_LIMIT),
        name="conf_layer",
    )(x, x, x, *consts)


def _route_plan(eidx, lrank, ts):
    t = eidx.shape[1]
    n_tiles = t // ts
    bm = MOE_BLOCK
    length = ((2 * t + N_EXPERTS * bm + bm - 1) // bm) * bm
    nb = length // bm
    experts = jnp.arange(N_EXPERTS, dtype=jnp.int32)
    onehot = (eidx.reshape(2, n_tiles, ts)[..., None] == experts).astype(jnp.int32)
    cnt = jnp.sum(onehot, axis=(0, 2))
    tile_base = jnp.cumsum(cnt, axis=0) - cnt
    counts = jnp.sum(cnt, axis=0)
    padded = ((counts + bm - 1) // bm) * bm
    pad_end = jnp.cumsum(padded)
    pad_start = pad_end - padded
    base = (pad_start[None, :] + tile_base).reshape(-1)
    tile_of = jnp.arange(t, dtype=jnp.int32) // ts
    pos = (base[tile_of[None, :] * N_EXPERTS + eidx] + lrank).astype(jnp.int32)
    blk_start = jnp.arange(nb, dtype=jnp.int32) * bm
    block_e = jnp.minimum(jnp.searchsorted(pad_end, blk_start, side="right"),
                          N_EXPERTS - 1).astype(jnp.int32)
    n_used = (pad_end[-1] // bm).astype(jnp.int32).reshape(1)
    pos_tiles = pos.reshape(2, n_tiles, ts).transpose(1, 0, 2).reshape(n_tiles, 1, 2 * ts)
    tail = n_used[0] + experts
    zero_off = jnp.concatenate([pad_end - bm, tail * bm])
    zero_on = jnp.concatenate([padded > 0, tail < nb])
    zero_off = jnp.where(zero_on, zero_off, -1).astype(jnp.int32)
    return dict(pos=pos, pos_tiles=pos_tiles, block_e=block_e, n_used=n_used,
                zero_off=zero_off, length=length)


def _window_tokens(t, ts, halo):
    n_tiles = t // ts
    w = ts + 2 * halo
    tok = jnp.arange(n_tiles, dtype=jnp.int32)[:, None] * ts - halo + jnp.arange(w, dtype=jnp.int32)[None, :]
    return jnp.clip(tok, 0, t - 1)


def _dispatch_body(zoff_ref, pos_ref, h_ref, hs_hbm, zbuf, sem, zsem):
    g8, sub, _ = h_ref.shape
    ts = g8 * sub
    bm = zbuf.shape[0]
    n_zero = zoff_ref.shape[0]

    @pl.when(pl.program_id(0) == 0)
    def _():
        zbuf[...] = jnp.zeros_like(zbuf)

        def zero_copy(j):
            off = pl.multiple_of(zoff_ref[j], bm)
            return pltpu.make_async_copy(zbuf, hs_hbm.at[pl.ds(off, bm), :], zsem)

        def start(j, carry):
            @pl.when(zoff_ref[j] >= 0)
            def _():
                zero_copy(j).start()
            return carry

        def wait(j, carry):
            @pl.when(zoff_ref[j] >= 0)
            def _():
                zero_copy(j).wait()
            return carry

        lax.fori_loop(0, n_zero, start, 0)
        lax.fori_loop(0, n_zero, wait, 0)

    def body(r8, carry):
        for k in range(2):
            for u in range(sub):
                p = pos_ref[0, 0, k * ts + r8 * sub + u]
                pltpu.make_async_copy(h_ref.at[r8, pl.ds(u, 1), :], hs_hbm.at[pl.ds(p, 1), :], sem).start()
        return carry

    lax.fori_loop(0, g8, body, 0)
    for _ in range(2):
        pltpu.make_async_copy(hs_hbm.at[pl.ds(0, ts), :], hs_hbm.at[pl.ds(ts, ts), :], sem).wait()


def _dispatch(h, plan):
    t, d = h.shape
    pos_tiles = plan["pos_tiles"]
    n_tiles = pos_tiles.shape[0]
    ts = t // n_tiles
    g8 = ts // SUBLANES
    grid_spec = pltpu.PrefetchScalarGridSpec(
        num_scalar_prefetch=1,
        grid=(n_tiles,),
        in_specs=[
            pl.BlockSpec((1, 1, 2 * ts), lambda i, z: (i, 0, 0), memory_space=pltpu.SMEM),
            pl.BlockSpec((g8, SUBLANES, d), lambda i, z: (i, 0, 0)),
        ],
        out_specs=pl.BlockSpec(memory_space=pl.ANY),
        scratch_shapes=[pltpu.VMEM((MOE_BLOCK, d), f32), pltpu.SemaphoreType.DMA(()),
                        pltpu.SemaphoreType.DMA(())],
    )
    return pl.pallas_call(
        _dispatch_body,
        grid_spec=grid_spec,
        out_shape=jax.ShapeDtypeStruct((plan["length"], d), f32),
        compiler_params=pltpu.CompilerParams(dimension_semantics=("arbitrary",)),
        name="dispatch",
    )(plan["zero_off"], pos_tiles, h.reshape(t // SUBLANES, SUBLANES, d))


def _moe_body(be_ref, nused_ref, hs_ref, wgu_ref, wd_ref, y_ref):
    d_exp = wd_ref.shape[1]
    b = pl.program_id(0)

    @pl.when(b < nused_ref[0])
    def _():
        x = hs_ref[...].astype(bf16)
        gu = jnp.dot(x, wgu_ref[0], preferred_element_type=f32)
        gt = gu[:, :d_exp]
        hid = gt * jax.nn.sigmoid(gt) * gu[:, d_exp:]
        y_ref[...] = jnp.dot(hid.astype(bf16), wd_ref[0], preferred_element_type=f32)

    @pl.when(b >= nused_ref[0])
    def _():
        y_ref[...] = jnp.zeros_like(y_ref)


def _moe(hs, plan, wgu, wd):
    length, d = hs.shape
    bm = MOE_BLOCK
    nb = length // bm
    d_gu = wgu.shape[2]
    d_exp = wd.shape[1]
    grid_spec = pltpu.PrefetchScalarGridSpec(
        num_scalar_prefetch=2,
        grid=(nb,),
        in_specs=[
            pl.BlockSpec((bm, d), lambda b, be, nu: (jnp.minimum(b, nu[0] - 1), 0)),
            pl.BlockSpec((1, d, d_gu), lambda b, be, nu: (be[b], 0, 0)),
            pl.BlockSpec((1, d_exp, d), lambda b, be, nu: (be[b], 0, 0)),
        ],
        out_specs=pl.BlockSpec((bm, d), lambda b, be, nu: (b, 0)),
    )
    return pl.pallas_call(
        _moe_body,
        grid_spec=grid_spec,
        out_shape=jax.ShapeDtypeStruct((length, d), f32),
        compiler_params=pltpu.CompilerParams(
            dimension_semantics=("arbitrary",), vmem_limit_bytes=VMEM_LIMIT),
        name="moe",
    )(plan["block_e"], plan["n_used"], hs, wgu, wd)


def _combine_gather(i, n_tiles, pos0_ref, posn_ref, y_hbm, gbuf, sem):
    _, n8, sub, d = gbuf.shape

    def issue(idx_ref, slot):
        def body(r8, carry):
            for u in range(sub):
                p = idx_ref[0, 0, r8 * sub + u]
                pltpu.make_async_copy(y_hbm.at[pl.ds(p, 1), :], gbuf.at[slot, r8, pl.ds(u, 1), :],
                                      sem.at[slot]).start()
            return carry
        lax.fori_loop(0, n8, body, 0)

    @pl.when(i == 0)
    def _():
        issue(pos0_ref, 0)

    @pl.when(i + 1 < n_tiles)
    def _():
        issue(posn_ref, (i + 1) % 2)

    slot = i % 2
    pltpu.make_async_copy(gbuf.at[slot], gbuf.at[slot], sem.at[slot]).wait()
    return gbuf[slot].reshape(n8 * sub, d)


def _sc_body(pos0_ref, posn_ref, xprev_ref, xmain_ref, xnext_ref, gw_ref, y_hbm, nmix_ref, win_ref,
             wdw_ref, wout_ref, nffn_ref, wra_ref, wrb_ref, tri_ref,
             x2_ref, h_ref, eidx_ref, gate_ref, lrank_ref, gbuf, sem, xin_scr, cu_scr,
             *, n_tiles, n_tiles_a, tiles_per_seq_a, tiles_per_seq_b):
    ts, d = xmain_ref.shape
    halo = SC_HALO
    w = ts + 2 * halo
    n_col = d // LANES
    i = pl.program_id(0)
    is_start, is_end = _seq_edges(i, n_tiles_a, tiles_per_seq_a, tiles_per_seq_b)
    yrows = _combine_gather(i, n_tiles, pos0_ref, posn_ref, y_hbm, gbuf, sem)

    xin_scr[0:halo, :] = xprev_ref[...]
    xin_scr[halo:halo + ts, :] = xmain_ref[...]
    xin_scr[halo + ts:w, :] = xnext_ref[...]
    gw = gw_ref[0]
    xc = xin_scr[...] + gw[:, 0:1] * yrows[0:w] + gw[:, 1:2] * yrows[w:2 * w]
    hn = _rms(xc, nmix_ref[...]).astype(bf16)
    z = jnp.dot(hn, win_ref[...], preferred_element_type=f32)
    cu = z[:, d:2 * d] * z[:, 2 * d:]
    rows = lax.broadcasted_iota(jnp.int32, (w, 1), 0)
    outside = (is_start & (rows < halo)) | (is_end & (rows >= halo + ts))
    cu = jnp.where(outside, 0.0, cu)
    for j in range(n_col):
        cu_scr[j] = cu[:, j * LANES:(j + 1) * LANES]
    taps = wdw_ref.shape[0]
    off = halo - taps // 2
    parts = []
    for j in range(n_col):
        cs = slice(j * LANES, (j + 1) * LANES)
        wj = wdw_ref[:, cs]
        acc = cu_scr[j, pl.ds(off, ts), :] * wj[0:1, :]
        for k in range(1, taps):
            acc = acc + cu_scr[j, pl.ds(off + k, ts), :] * wj[k:k + 1, :]
        parts.append(acc)
    conv = jnp.concatenate(parts, axis=1)
    bv = z[halo:halo + ts, :d] * conv
    m = jnp.dot(bv.astype(bf16), wout_ref[...], preferred_element_type=f32)
    x2 = xc[halo:halo + ts, :] + m
    x2_ref[...] = x2
    h = _rms(x2, nffn_ref[...])
    h_ref[...] = h
    _route(h, wra_ref, wrb_ref, tri_ref, eidx_ref, gate_ref, lrank_ref)


def _sc_layer(x1, y, posw, gw, seq_layout, nmix, win, wdw, wout, nffn, wra, wrb, tri):
    t, d = x1.shape
    ts = TOKEN_TILE
    n_tiles = t // ts
    n_tiles_a, tiles_per_seq_a, tiles_per_seq_b = seq_layout
    hb = ts // SC_HALO
    n_hblk = t // SC_HALO
    w = ts + 2 * SC_HALO
    smem = pltpu.SMEM
    consts = (nmix, win, wdw, wout, nffn, wra, wrb, tri)
    in_specs = [
        pl.BlockSpec((1, 1, 2 * w), lambda i: (0, 0, 0), memory_space=smem),
        pl.BlockSpec((1, 1, 2 * w), lambda i: (jnp.minimum(i + 1, n_tiles - 1), 0, 0), memory_space=smem),
        pl.BlockSpec((SC_HALO, d), lambda i: (jnp.maximum(i * hb - 1, 0), 0)),
        pl.BlockSpec((ts, d), lambda i: (i, 0)),
        pl.BlockSpec((SC_HALO, d), lambda i: (jnp.minimum((i + 1) * hb, n_hblk - 1), 0)),
        pl.BlockSpec((1, w, 2), lambda i: (i, 0, 0)),
        pl.BlockSpec(memory_space=pl.ANY),
    ] + [_const_spec(a.shape) for a in consts]
    out_specs, out_shape = _layer_out(t, d, ts)
    body = functools.partial(_sc_body, n_tiles=n_tiles, n_tiles_a=n_tiles_a,
                             tiles_per_seq_a=tiles_per_seq_a, tiles_per_seq_b=tiles_per_seq_b)
    return pl.pallas_call(
        body,
        grid=(n_tiles,),
        in_specs=in_specs,
        out_specs=out_specs,
        out_shape=out_shape,
        scratch_shapes=[
            pltpu.VMEM((2, 2 * w // SUBLANES, SUBLANES, d), f32),
            pltpu.SemaphoreType.DMA((2,)),
            pltpu.VMEM((w, d), f32),
            pltpu.VMEM((d // LANES, w, LANES), f32),
        ],
        compiler_params=pltpu.CompilerParams(
            dimension_semantics=("arbitrary",), vmem_limit_bytes=VMEM_LIMIT),
        name="sc_layer",
    )(posw, posw, x1, x1, x1, gw, y, *consts)


def _final_body(pos0_ref, posn_ref, x_ref, gt_ref, y_hbm, g_ref, o_ref, gbuf, sem, *, n_tiles):
    ts = x_ref.shape[0]
    i = pl.program_id(0)
    yrows = _combine_gather(i, n_tiles, pos0_ref, posn_ref, y_hbm, gbuf, sem)
    gt = gt_ref[...]
    xc = x_ref[...] + gt[:, 0:1] * yrows[0:ts] + gt[:, 1:2] * yrows[ts:2 * ts]
    o_ref[...] = _rms(xc, g_ref[...])


def _final(x2, y, pos_tiles, gates_t, g, tile_off, n_tiles):
    t, d = x2.shape
    ts = TOKEN_TILE
    smem = pltpu.SMEM
    in_specs = [
        pl.BlockSpec((1, 1, 2 * ts), lambda i: (tile_off, 0, 0), memory_space=smem),
        pl.BlockSpec((1, 1, 2 * ts), lambda i: (tile_off + jnp.minimum(i + 1, n_tiles - 1), 0, 0),
                     memory_space=smem),
        pl.BlockSpec((ts, d), lambda i: (tile_off + i, 0)),
        pl.BlockSpec((ts, 2), lambda i: (tile_off + i, 0)),
        pl.BlockSpec(memory_space=pl.ANY),
        _const_spec(g.shape),
    ]
    return pl.pallas_call(
        functools.partial(_final_body, n_tiles=n_tiles),
        grid=(n_tiles,),
        in_specs=in_specs,
        out_specs=pl.BlockSpec((ts, d), lambda i: (i, 0)),
        out_shape=jax.ShapeDtypeStruct((n_tiles * ts, d), f32),
        scratch_shapes=[pltpu.VMEM((2, 2 * ts // SUBLANES, SUBLANES, d), f32),
                        pltpu.SemaphoreType.DMA((2,))],
        compiler_params=pltpu.CompilerParams(
            dimension_semantics=("arbitrary",), vmem_limit_bytes=VMEM_LIMIT),
        name="final_norm",
    )(pos_tiles, pos_tiles, x2, gates_t, y, g)


def _router_weights(w_rg, w_re):
    d = w_rg.shape[0]
    wt = jnp.concatenate([w_re.T, w_rg.T, jnp.zeros((ROUTER_ROWS - N_EXPERTS - N_GROUPS, d), f32)], axis=0)
    hi = wt.astype(bf16)
    lo = (wt - hi.astype(f32)).astype(bf16)
    return jnp.concatenate([hi, lo], axis=0), hi


def kernel(x_prompt, x_sample, norm_mix, conf_w_pw1, conf_b_pw1, conf_w_dw, conf_b_dw, conf_ln_g,
           conf_ln_b, conf_w_pw2, conf_b_pw2, sc_w_in, sc_w_dw, sc_w_out, norm_ffn, router_group,
           router_expert, w_gate, w_up, w_down, final_norm):
    bp, sp, d = x_prompt.shape
    bs, ss, _ = x_sample.shape
    ts = TOKEN_TILE
    assert norm_mix.shape[0] == 2 and sp % ts == 0 and ss % ts == 0 and d % LANES == 0
    tp, tsamp = bp * sp, bs * ss
    t = tp + tsamp
    seq_layout = (tp // ts, sp // ts, ss // ts)
    x = jnp.concatenate([x_prompt.reshape(tp, d), x_sample.reshape(tsamp, d)], axis=0)

    row = lambda a: a.reshape(1, -1)
    wgu = jnp.concatenate([w_gate, w_up], axis=-1).astype(bf16)
    wd = w_down.astype(bf16)
    wra0, wrb0 = _router_weights(router_group[0], router_expert[0])
    wra1, wrb1 = _router_weights(router_group[1], router_expert[1])
    ti = jnp.arange(ts, dtype=jnp.int32)
    tri = (ti[:, None] < ti[None, :]).astype(bf16)

    x1, h1, eidx1, gates1, lrank1 = _conf_layer(
        x, seq_layout, row(norm_mix[0]), conf_w_pw1[0].astype(bf16), row(conf_b_pw1[0]),
        conf_w_dw[0], row(conf_b_dw[0]), row(conf_ln_g[0]), row(conf_ln_b[0]),
        conf_w_pw2[0].astype(bf16), row(conf_b_pw2[0]), row(norm_ffn[0]), wra0, wrb0, tri)
    plan1 = _route_plan(eidx1, lrank1, ts)
    y1 = _moe(_dispatch(h1, plan1), plan1, wgu[0], wd[0])

    tokw = _window_tokens(t, ts, SC_HALO)
    posw1 = jnp.concatenate([plan1["pos"][0][tokw], plan1["pos"][1][tokw]], axis=1)
    posw1 = posw1.reshape(t // ts, 1, 2 * tokw.shape[1])
    gw1 = gates1.T[tokw]
    x2, h2, eidx2, gates2, lrank2 = _sc_layer(
        x1, y1, posw1, gw1, seq_layout, row(norm_mix[1]), sc_w_in[0].astype(bf16), sc_w_dw[0],
        sc_w_out[0].astype(bf16), row(norm_ffn[1]), wra1, wrb1, tri)
    plan2 = _route_plan(eidx2, lrank2, ts)
    y2 = _moe(_dispatch(h2, plan2), plan2, wgu[1], wd[1])

    fg = row(final_norm)
    gates2_t = gates2.T
    out_p = _final(x2, y2, plan2["pos_tiles"], gates2_t, fg, 0, tp // ts)
    out_s = _final(x2, y2, plan2["pos_tiles"], gates2_t, fg, tp // ts, tsamp // ts)
    return out_p.reshape(bp, sp, d), out_s.reshape(bs, ss, d)
```

```python
import functools

import jax
import jax.numpy as jnp
from jax import lax
from jax.experimental import pallas as pl
from jax.experimental.pallas import tpu as pltpu

RMS_EPS = 1e-6
LN_EPS = 1e-5
N_GROUPS = 4
EXPERTS_PER_GROUP = 8
N_EXPERTS = N_GROUPS * EXPERTS_PER_GROUP
ROUTER_ROWS = 40
LANES = 128
SUBLANES = 8
TOKEN_TILE = 512
CONF_HALO = 16
SC_HALO = 8
MOE_BLOCK = 256
CONV_ROWS = 128
ROW_DMA_UNROLL = 8
VMEM_LIMIT = 56 * 1024 * 1024

f32 = jnp.float32
bf16 = jnp.bfloat16


def _rms(x, g):
    ms = jnp.mean(x * x, axis=-1, keepdims=True)
    return x * lax.rsqrt(ms + RMS_EPS) * g


def _route(h, wra_ref, wrb_ref, tri_ref, eidx_ref, gate_ref, lrank_ref):
    rows = h.shape[0]
    h_hi = h.astype(bf16)
    h_lo = (h - h_hi.astype(f32)).astype(bf16)
    dn = (((1,), (1,)), ((), ()))
    la = lax.dot_general(wra_ref[...], h_hi, dn, preferred_element_type=f32)
    lb = lax.dot_general(wrb_ref[...], h_lo, dn, preferred_element_type=f32)
    lt = la[:ROUTER_ROWS] + la[ROUTER_ROWS:] + lb
    el = lt[:N_EXPERTS]
    gl = lt[N_EXPERTS:N_EXPERTS + N_GROUPS]
    gm = jnp.max(gl, axis=0, keepdims=True)
    gp = jnp.exp(gl - gm)
    gprob = gp / jnp.sum(gp, axis=0, keepdims=True)
    g_top_p = jnp.max(gprob, axis=0, keepdims=True)
    iota_g = lax.broadcasted_iota(jnp.int32, (N_GROUPS, rows), 0)
    g_top = jnp.min(jnp.where(gprob == g_top_p, iota_g, N_GROUPS), axis=0, keepdims=True)
    elg = jnp.zeros((EXPERTS_PER_GROUP, rows), f32)
    for g in range(N_GROUPS):
        sl = el[g * EXPERTS_PER_GROUP:(g + 1) * EXPERTS_PER_GROUP]
        elg = elg + jnp.where(g_top == g, sl, 0.0)
    em = jnp.max(elg, axis=0, keepdims=True)
    ep = jnp.exp(elg - em)
    eprob = ep / jnp.sum(ep, axis=0, keepdims=True)
    iota_e = lax.broadcasted_iota(jnp.int32, (EXPERTS_PER_GROUP, rows), 0)
    p1 = jnp.max(eprob, axis=0, keepdims=True)
    i1 = jnp.min(jnp.where(eprob == p1, iota_e, EXPERTS_PER_GROUP), axis=0, keepdims=True)
    rest = jnp.where(iota_e == i1, -1.0, eprob)
    p2 = jnp.max(rest, axis=0, keepdims=True)
    i2 = jnp.min(jnp.where(rest == p2, iota_e, EXPERTS_PER_GROUP), axis=0, keepdims=True)
    den = p1 + p2
    e1 = g_top * EXPERTS_PER_GROUP + i1
    e2 = g_top * EXPERTS_PER_GROUP + i2
    eidx_ref[0:1, :] = e1
    eidx_ref[1:2, :] = e2
    gate_ref[0:1, :] = g_top_p * (p1 / den)
    gate_ref[1:2, :] = g_top_p * (p2 / den)

    iota_x = lax.broadcasted_iota(jnp.int32, (N_EXPERTS, rows), 0)
    oh = jnp.concatenate([iota_x == e1, iota_x == e2], axis=0).astype(f32)
    before = jnp.dot(oh.astype(bf16), tri_ref[...], preferred_element_type=f32)
    cnt1 = jnp.sum(oh[:N_EXPERTS], axis=1, keepdims=True)
    r1 = jnp.sum(oh[:N_EXPERTS] * before[:N_EXPERTS], axis=0, keepdims=True)
    r2 = jnp.sum(oh[N_EXPERTS:] * (before[N_EXPERTS:] + cnt1), axis=0, keepdims=True)
    lrank_ref[0:1, :] = r1.astype(jnp.int32)
    lrank_ref[1:2, :] = r2.astype(jnp.int32)


def _seq_edges(i, n_tiles_a, tiles_per_seq_a, tiles_per_seq_b):
    in_a = i < n_tiles_a
    per = jnp.where(in_a, tiles_per_seq_a, tiles_per_seq_b)
    j = jnp.where(in_a, i, i - n_tiles_a) % per
    return j == 0, j == per - 1


def _const_spec(shape):
    nd = len(shape)
    return pl.BlockSpec(shape, lambda i: (0,) * nd, pipeline_mode=pl.Buffered(1))


def _row_tiles(x):
    return x.reshape(x.shape[0], x.shape[1] // LANES, LANES)


def _layer_out(t, d, ts):
    out_specs = [
        pl.BlockSpec((ts, d), lambda i: (i, 0)),
        pl.BlockSpec((ts, d // LANES, LANES), lambda i: (i, 0, 0)),
        pl.BlockSpec((2, ts), lambda i: (0, i)),
        pl.BlockSpec((2, ts), lambda i: (0, i)),
        pl.BlockSpec((2, ts), lambda i: (0, i)),
    ]
    out_shape = [
        jax.ShapeDtypeStruct((t, d), f32),
        jax.ShapeDtypeStruct((t, d // LANES, LANES), bf16),
        jax.ShapeDtypeStruct((2, t), jnp.int32),
        jax.ShapeDtypeStruct((2, t), f32),
        jax.ShapeDtypeStruct((2, t), jnp.int32),
    ]
    return out_specs, out_shape


def _conf_body(aprev_ref, amain_ref, anext_ref, bprev_ref, bmain_ref, bnext_ref,
               nmix_ref, w1_ref, b1_ref, wdw_ref, bdw_ref,
               lng_ref, lnb_ref, w2_ref, b2_ref, nffn_ref, wra_ref, wrb_ref, tri_ref,
               x1_ref, h_ref, eidx_ref, gate_ref, lrank_ref, xn_scr, v_scr, u_scr,
               *, n_tiles_a, tiles_per_seq_a, tiles_per_seq_b):
    ts, d = amain_ref.shape
    halo = CONF_HALO
    w = ts + 2 * halo
    n_col = d // LANES
    taps = wdw_ref.shape[0]
    i = pl.program_id(0)
    is_start, is_end = _seq_edges(i, n_tiles_a, tiles_per_seq_a, tiles_per_seq_b)
    in_a = i < n_tiles_a
    xmain = jnp.where(in_a, amain_ref[...], bmain_ref[...])

    g = nmix_ref[...]
    xn_scr[0:halo, :] = _rms(jnp.where(in_a, aprev_ref[...], bprev_ref[...]), g).astype(bf16)
    xn_scr[halo:halo + ts, :] = _rms(xmain, g).astype(bf16)
    xn_scr[halo + ts:w, :] = _rms(jnp.where(in_a, anext_ref[...], bnext_ref[...]), g).astype(bf16)
    z = jnp.dot(xn_scr[...], w1_ref[...], preferred_element_type=f32) + b1_ref[...]
    v = z[:, :d] * jax.nn.sigmoid(z[:, d:])
    rows = lax.broadcasted_iota(jnp.int32, (w, 1), 0)
    outside = (is_start & (rows < halo)) | (is_end & (rows >= halo + ts))
    v = jnp.where(outside, 0.0, v)
    for j in range(n_col):
        v_scr[j] = v[:, j * LANES:(j + 1) * LANES]

    off = halo - taps // 2
    for j in range(n_col):
        cs = slice(j * LANES, (j + 1) * LANES)
        wj = wdw_ref[:, cs]
        wb = [jnp.broadcast_to(wj[k:k + 1, :], (CONV_ROWS, LANES)) for k in range(taps)]
        bj = bdw_ref[:, cs]
        for c in range(ts // CONV_ROWS):
            acc = v_scr[j, pl.ds(c * CONV_ROWS + off, CONV_ROWS), :] * wb[0]
            for k in range(1, taps):
                acc = acc + v_scr[j, pl.ds(c * CONV_ROWS + off + k, CONV_ROWS), :] * wb[k]
            u_scr[pl.ds(c * CONV_ROWS, CONV_ROWS), cs] = acc + bj

    u = u_scr[...]
    mu = jnp.mean(u, axis=-1, keepdims=True)
    uc = u - mu
    var = jnp.mean(uc * uc, axis=-1, keepdims=True)
    y = uc * lax.rsqrt(var + LN_EPS) * lng_ref[...] + lnb_ref[...]
    s = y * jax.nn.sigmoid(y)
    m = jnp.dot(s.astype(bf16), w2_ref[...], preferred_element_type=f32) + b2_ref[...]
    x1 = jnp.where(in_a, amain_ref[...], bmain_ref[...]) + m
    x1_ref[...] = x1
    h = _rms(x1, nffn_ref[...])
    h_ref[...] = _row_tiles(h.astype(bf16))
    _route(h, wra_ref, wrb_ref, tri_ref, eidx_ref, gate_ref, lrank_ref)


def _halo_specs(n_rows, d, ts, halo, tile_of_step):
    hb = ts // halo
    last = n_rows // halo - 1
    return [
        pl.BlockSpec((halo, d), lambda i: (jnp.maximum(tile_of_step(i) * hb - 1, 0), 0)),
        pl.BlockSpec((ts, d), lambda i: (tile_of_step(i), 0)),
        pl.BlockSpec((halo, d), lambda i: (jnp.minimum((tile_of_step(i) + 1) * hb, last), 0)),
    ]


def _conf_layer(xa, xb, seq_layout, nmix, w1, b1, wdw, bdw, lng, lnb, w2, b2, nffn, wra, wrb, tri):
    d = xa.shape[1]
    t = xa.shape[0] + xb.shape[0]
    ts = TOKEN_TILE
    n_tiles = t // ts
    n_tiles_a, tiles_per_seq_a, tiles_per_seq_b = seq_layout
    consts = (nmix, w1, b1, wdw, bdw, lng, lnb, w2, b2, nffn, wra, wrb, tri)
    in_specs = (
        _halo_specs(xa.shape[0], d, ts, CONF_HALO, lambda i: jnp.minimum(i, n_tiles_a - 1))
        + _halo_specs(xb.shape[0], d, ts, CONF_HALO, lambda i: jnp.maximum(i - n_tiles_a, 0))
        + [_const_spec(a.shape) for a in consts])
    out_specs, out_shape = _layer_out(t, d, ts)
    w = ts + 2 * CONF_HALO
    body = functools.partial(_conf_body, n_tiles_a=n_tiles_a, tiles_per_seq_a=tiles_per_seq_a,
                             tiles_per_seq_b=tiles_per_seq_b)
    return pl.pallas_call(
        body,
        grid=(n_tiles,),
        in_specs=in_specs,
        out_specs=out_specs,
        out_shape=out_shape,
        scratch_shapes=[
            pltpu.VMEM((w, d), bf16),
            pltpu.VMEM((d // LANES, w, LANES), f32),
            pltpu.VMEM((ts, d), f32),
        ],
        compiler_params=pltpu.CompilerParams(
            dimension_semantics=("arbitrary",), vmem_limit_bytes=VMEM_LIMIT),
        name="conf_layer",
    )(xa, xa, xa, xb, xb, xb, *consts)


def _route_plan(eidx, lrank, ts):
    t = eidx.shape[1]
    n_tiles = t // ts
    bm = MOE_BLOCK
    length = ((2 * t + N_EXPERTS * bm + bm - 1) // bm) * bm
    nb = length // bm
    experts = jnp.arange(N_EXPERTS, dtype=jnp.int32)
    tiles = jnp.arange(n_tiles, dtype=jnp.int32)
    eidx3 = eidx.reshape(2, n_tiles, ts)
    onehot = eidx3[..., None] == experts
    cnt = jnp.sum(onehot.astype(jnp.int32), axis=(0, 2))
    earlier = tiles[:, None] > tiles[None, :]
    tile_base = jnp.sum(jnp.where(earlier[:, :, None], cnt[None, :, :], 0), axis=1)
    counts = jnp.sum(cnt, axis=0)
    padded = ((counts + bm - 1) // bm) * bm
    pad_end = jnp.sum(jnp.where(experts[:, None] >= experts[None, :], padded[None, :], 0), axis=1)
    pad_start = pad_end - padded
    base = pad_start[None, :] + tile_base
    pos3 = jnp.sum(jnp.where(onehot, base[None, :, None, :], 0), axis=-1) + lrank.reshape(2, n_tiles, ts)
    pos = pos3.reshape(2, t).astype(jnp.int32)
    blk_start = jnp.arange(nb, dtype=jnp.int32) * bm
    block_e = jnp.minimum(jnp.sum((blk_start[:, None] >= pad_end[None, :]).astype(jnp.int32), axis=1),
                          N_EXPERTS - 1).astype(jnp.int32)
    n_used = (pad_end[-1] // bm).astype(jnp.int32).reshape(1)
    pos_tiles = pos.reshape(2, n_tiles, ts).transpose(1, 0, 2).reshape(n_tiles, 1, 2 * ts)
    tail = n_used[0] + experts
    zero_off = jnp.concatenate([pad_end - bm, tail * bm])
    zero_on = jnp.concatenate([padded > 0, tail < nb])
    zero_off = jnp.where(zero_on, zero_off, -1).astype(jnp.int32)
    return dict(pos=pos, pos_tiles=pos_tiles, block_e=block_e, n_used=n_used,
                zero_off=zero_off, length=length)


def _window_tokens(t, ts, halo):
    n_tiles = t // ts
    w = ts + 2 * halo
    tok = jnp.arange(n_tiles, dtype=jnp.int32)[:, None] * ts - halo + jnp.arange(w, dtype=jnp.int32)[None, :]
    return jnp.clip(tok, 0, t - 1)


def _dispatch_body(zoff_ref, pos_ref, h_ref, hs_hbm, zbuf, sem, zsem):
    ts = h_ref.shape[0]
    bm = zbuf.shape[0]
    n_zero = zoff_ref.shape[0]

    @pl.when(pl.program_id(0) == 0)
    def _():
        zbuf[...] = jnp.zeros_like(zbuf)

        def zero_copy(j):
            off = pl.multiple_of(zoff_ref[j], bm)
            return pltpu.make_async_copy(zbuf, hs_hbm.at[pl.ds(off, bm)], zsem)

        def start(j, carry):
            @pl.when(zoff_ref[j] >= 0)
            def _():
                zero_copy(j).start()
            return carry

        def wait(j, carry):
            @pl.when(zoff_ref[j] >= 0)
            def _():
                zero_copy(j).wait()
            return carry

        lax.fori_loop(0, n_zero, start, 0)
        lax.fori_loop(0, n_zero, wait, 0)

    def body(r8, carry):
        for u in range(ROW_DMA_UNROLL):
            r = r8 * ROW_DMA_UNROLL + u
            for k in range(2):
                p = pos_ref[0, 0, k * ts + r]
                pltpu.make_async_copy(h_ref.at[r], hs_hbm.at[p], sem).start(priority=(2 * u + k) % 2)
        return carry

    lax.fori_loop(0, ts // ROW_DMA_UNROLL, body, 0)
    for _ in range(2):
        pltpu.make_async_copy(hs_hbm.at[pl.ds(0, ts)], hs_hbm.at[pl.ds(ts, ts)], sem).wait()


def _dispatch(h, plan):
    t, n_sub, n_lane = h.shape
    pos_tiles = plan["pos_tiles"]
    n_tiles = pos_tiles.shape[0]
    ts = t // n_tiles
    grid_spec = pltpu.PrefetchScalarGridSpec(
        num_scalar_prefetch=1,
        grid=(n_tiles,),
        in_specs=[
            pl.BlockSpec((1, 1, 2 * ts), lambda i, z: (i, 0, 0), memory_space=pltpu.SMEM),
            pl.BlockSpec((ts, n_sub, n_lane), lambda i, z: (i, 0, 0)),
        ],
        out_specs=pl.BlockSpec(memory_space=pl.ANY),
        scratch_shapes=[pltpu.VMEM((MOE_BLOCK, n_sub, n_lane), h.dtype), pltpu.SemaphoreType.DMA(()),
                        pltpu.SemaphoreType.DMA(())],
    )
    return pl.pallas_call(
        _dispatch_body,
        grid_spec=grid_spec,
        out_shape=jax.ShapeDtypeStruct((plan["length"], n_sub, n_lane), h.dtype),
        compiler_params=pltpu.CompilerParams(dimension_semantics=("arbitrary",)),
        name="dispatch",
    )(plan["zero_off"], pos_tiles, h)


def _moe_body(be_ref, nused_ref, hs_ref, wgu_ref, wd_ref, y_ref):
    d_exp = wd_ref.shape[1]
    b = pl.program_id(0)

    @pl.when(b < nused_ref[0])
    def _():
        bm, n_sub, n_lane = hs_ref.shape
        x = hs_ref[...].reshape(bm, n_sub * n_lane)
        gu = jnp.dot(x, wgu_ref[0], preferred_element_type=f32)
        gt = gu[:, :d_exp]
        hid = gt * jax.nn.sigmoid(gt) * gu[:, d_exp:]
        y = jnp.dot(hid.astype(bf16), wd_ref[0], preferred_element_type=f32)
        y_ref[...] = _row_tiles(y)

    @pl.when(b >= nused_ref[0])
    def _():
        y_ref[...] = jnp.zeros_like(y_ref)


def _moe(hs, plan, wgu, wd):
    length, n_sub, n_lane = hs.shape
    d = n_sub * n_lane
    bm = MOE_BLOCK
    nb = length // bm
    d_gu = wgu.shape[2]
    d_exp = wd.shape[1]
    grid_spec = pltpu.PrefetchScalarGridSpec(
        num_scalar_prefetch=2,
        grid=(nb,),
        in_specs=[
            pl.BlockSpec((bm, n_sub, n_lane), lambda b, be, nu: (jnp.minimum(b, nu[0] - 1), 0, 0)),
            pl.BlockSpec((1, d, d_gu), lambda b, be, nu: (be[b], 0, 0)),
            pl.BlockSpec((1, d_exp, d), lambda b, be, nu: (be[b], 0, 0)),
        ],
        out_specs=pl.BlockSpec((bm, n_sub, n_lane), lambda b, be, nu: (b, 0, 0)),
    )
    return pl.pallas_call(
        _moe_body,
        grid_spec=grid_spec,
        out_shape=jax.ShapeDtypeStruct((length, n_sub, n_lane), f32),
        compiler_params=pltpu.CompilerParams(
            dimension_semantics=("arbitrary",), vmem_limit_bytes=VMEM_LIMIT),
        name="moe",
    )(plan["block_e"], plan["n_used"], hs, wgu, wd)


def _combine_gather(i, n_tiles, pos0_ref, posn_ref, y_hbm, gbuf, sem):
    _, n_rows, n_sub, n_lane = gbuf.shape

    def issue(idx_ref, slot):
        def body(r8, carry):
            for u in range(ROW_DMA_UNROLL):
                r = r8 * ROW_DMA_UNROLL + u
                p = idx_ref[0, 0, r]
                pltpu.make_async_copy(y_hbm.at[p], gbuf.at[slot, r], sem.at[slot]).start(priority=u % 2)
            return carry
        lax.fori_loop(0, n_rows // ROW_DMA_UNROLL, body, 0)

    @pl.when(i == 0)
    def _():
        issue(pos0_ref, 0)

    @pl.when(i + 1 < n_tiles)
    def _():
        issue(posn_ref, (i + 1) % 2)

    slot = i % 2
    pltpu.make_async_copy(gbuf.at[slot], gbuf.at[slot], sem.at[slot]).wait()
    return gbuf[slot].reshape(n_rows, n_sub * n_lane)


def _sc_body(pos0_ref, posn_ref, xprev_ref, xmain_ref, xnext_ref, gw_ref, y_hbm, nmix_ref, win_ref,
             wdw_ref, wout_ref, nffn_ref, wra_ref, wrb_ref, tri_ref,
             x2_ref, h_ref, eidx_ref, gate_ref, lrank_ref, gbuf, sem, xin_scr, cu_scr,
             *, n_tiles, n_tiles_a, tiles_per_seq_a, tiles_per_seq_b):
    ts, d = xmain_ref.shape
    halo = SC_HALO
    w = ts + 2 * halo
    n_col = d // LANES
    i = pl.program_id(0)
    is_start, is_end = _seq_edges(i, n_tiles_a, tiles_per_seq_a, tiles_per_seq_b)
    yrows = _combine_gather(i, n_tiles, pos0_ref, posn_ref, y_hbm, gbuf, sem)

    xin_scr[0:halo, :] = xprev_ref[...]
    xin_scr[halo:halo + ts, :] = xmain_ref[...]
    xin_scr[halo + ts:w, :] = xnext_ref[...]
    gw = gw_ref[0]
    xc = xin_scr[...] + gw[:, 0:1] * yrows[0:w] + gw[:, 1:2] * yrows[w:2 * w]
    hn = _rms(xc, nmix_ref[...]).astype(bf16)
    z = jnp.dot(hn, win_ref[...], preferred_element_type=f32)
    cu = z[:, d:2 * d] * z[:, 2 * d:]
    rows = lax.broadcasted_iota(jnp.int32, (w, 1), 0)
    outside = (is_start & (rows < halo)) | (is_end & (rows >= halo + ts))
    cu = jnp.where(outside, 0.0, cu)
    for j in range(n_col):
        cu_scr[j] = cu[:, j * LANES:(j + 1) * LANES]
    taps = wdw_ref.shape[0]
    off = halo - taps // 2
    parts = []
    for j in range(n_col):
        cs = slice(j * LANES, (j + 1) * LANES)
        wj = wdw_ref[:, cs]
        acc = cu_scr[j, pl.ds(off, ts), :] * wj[0:1, :]
        for k in range(1, taps):
            acc = acc + cu_scr[j, pl.ds(off + k, ts), :] * wj[k:k + 1, :]
        parts.append(acc)
    conv = jnp.concatenate(parts, axis=1)
    bv = z[halo:halo + ts, :d] * conv
    m = jnp.dot(bv.astype(bf16), wout_ref[...], preferred_element_type=f32)
    x2 = xc[halo:halo + ts, :] + m
    x2_ref[...] = x2
    h = _rms(x2, nffn_ref[...])
    h_ref[...] = _row_tiles(h.astype(bf16))
    _route(h, wra_ref, wrb_ref, tri_ref, eidx_ref, gate_ref, lrank_ref)


def _sc_layer(x1, y, posw, gw, seq_layout, nmix, win, wdw, wout, nffn, wra, wrb, tri):
    t, d = x1.shape
    ts = TOKEN_TILE
    n_tiles = t // ts
    n_tiles_a, tiles_per_seq_a, tiles_per_seq_b = seq_layout
    hb = ts // SC_HALO
    n_hblk = t // SC_HALO
    w = ts + 2 * SC_HALO
    smem = pltpu.SMEM
    consts = (nmix, win, wdw, wout, nffn, wra, wrb, tri)
    in_specs = [
        pl.BlockSpec((1, 1, 2 * w), lambda i: (0, 0, 0), memory_space=smem),
        pl.BlockSpec((1, 1, 2 * w), lambda i: (jnp.minimum(i + 1, n_tiles - 1), 0, 0), memory_space=smem),
        pl.BlockSpec((SC_HALO, d), lambda i: (jnp.maximum(i * hb - 1, 0), 0)),
        pl.BlockSpec((ts, d), lambda i: (i, 0)),
        pl.BlockSpec((SC_HALO, d), lambda i: (jnp.minimum((i + 1) * hb, n_hblk - 1), 0)),
        pl.BlockSpec((1, w, 2), lambda i: (i, 0, 0)),
        pl.BlockSpec(memory_space=pl.ANY),
    ] + [_const_spec(a.shape) for a in consts]
    out_specs, out_shape = _layer_out(t, d, ts)
    body = functools.partial(_sc_body, n_tiles=n_tiles, n_tiles_a=n_tiles_a,
                             tiles_per_seq_a=tiles_per_seq_a, tiles_per_seq_b=tiles_per_seq_b)
    return pl.pallas_call(
        body,
        grid=(n_tiles,),
        in_specs=in_specs,
        out_specs=out_specs,
        out_shape=out_shape,
        scratch_shapes=[
            pltpu.VMEM((2, 2 * w, d // LANES, LANES), f32),
            pltpu.SemaphoreType.DMA((2,)),
            pltpu.VMEM((w, d), f32),
            pltpu.VMEM((d // LANES, w, LANES), f32),
        ],
        compiler_params=pltpu.CompilerParams(
            dimension_semantics=("arbitrary",), vmem_limit_bytes=VMEM_LIMIT),
        name="sc_layer",
    )(posw, posw, x1, x1, x1, gw, y, *consts)


def _final_body(pos0_ref, posn_ref, x_ref, gt_ref, y_hbm, g_ref, o_ref, gbuf, sem, *, n_tiles):
    ts = x_ref.shape[0]
    i = pl.program_id(0)
    yrows = _combine_gather(i, n_tiles, pos0_ref, posn_ref, y_hbm, gbuf, sem)
    gt = gt_ref[...]
    xc = x_ref[...] + gt[:, 0:1] * yrows[0:ts] + gt[:, 1:2] * yrows[ts:2 * ts]
    o_ref[...] = _rms(xc, g_ref[...])


def _final(x2, y, pos_tiles, gates_t, g, tile_off, n_tiles):
    t, d = x2.shape
    ts = TOKEN_TILE
    smem = pltpu.SMEM
    in_specs = [
        pl.BlockSpec((1, 1, 2 * ts), lambda i: (tile_off, 0, 0), memory_space=smem),
        pl.BlockSpec((1, 1, 2 * ts), lambda i: (tile_off + jnp.minimum(i + 1, n_tiles - 1), 0, 0),
                     memory_space=smem),
        pl.BlockSpec((ts, d), lambda i: (tile_off + i, 0)),
        pl.BlockSpec((ts, 2), lambda i: (tile_off + i, 0)),
        pl.BlockSpec(memory_space=pl.ANY),
        _const_spec(g.shape),
    ]
    return pl.pallas_call(
        functools.partial(_final_body, n_tiles=n_tiles),
        grid=(n_tiles,),
        in_specs=in_specs,
        out_specs=pl.BlockSpec((ts, d), lambda i: (i, 0)),
        out_shape=jax.ShapeDtypeStruct((n_tiles * ts, d), f32),
        scratch_shapes=[pltpu.VMEM((2, 2 * ts, d // LANES, LANES), f32),
                        pltpu.SemaphoreType.DMA((2,))],
        compiler_params=pltpu.CompilerParams(
            dimension_semantics=("arbitrary",), vmem_limit_bytes=VMEM_LIMIT),
        name="final_norm",
    )(pos_tiles, pos_tiles, x2, gates_t, y, g)


def _router_weights(w_rg, w_re):
    d = w_rg.shape[0]
    wt = jnp.concatenate([w_re.T, w_rg.T, jnp.zeros((ROUTER_ROWS - N_EXPERTS - N_GROUPS, d), f32)], axis=0)
    hi = wt.astype(bf16)
    lo = (wt - hi.astype(f32)).astype(bf16)
    return jnp.concatenate([hi, lo], axis=0), hi


def kernel(x_prompt, x_sample, norm_mix, conf_w_pw1, conf_b_pw1, conf_w_dw, conf_b_dw, conf_ln_g,
           conf_ln_b, conf_w_pw2, conf_b_pw2, sc_w_in, sc_w_dw, sc_w_out, norm_ffn, router_group,
           router_expert, w_gate, w_up, w_down, final_norm):
    bp, sp, d = x_prompt.shape
    bs, ss, _ = x_sample.shape
    ts = TOKEN_TILE
    assert norm_mix.shape[0] == 2 and sp % ts == 0 and ss % ts == 0 and d % LANES == 0
    tp, tsamp = bp * sp, bs * ss
    t = tp + tsamp
    seq_layout = (tp // ts, sp // ts, ss // ts)

    row = lambda a: a.reshape(1, -1)
    wgu = jnp.concatenate([w_gate, w_up], axis=-1).astype(bf16)
    wd = w_down.astype(bf16)
    wra0, wrb0 = _router_weights(router_group[0], router_expert[0])
    wra1, wrb1 = _router_weights(router_group[1], router_expert[1])
    ti = jnp.arange(ts, dtype=jnp.int32)
    tri = (ti[:, None] < ti[None, :]).astype(bf16)

    x1, h1, eidx1, gates1, lrank1 = _conf_layer(
        x_prompt.reshape(tp, d), x_sample.reshape(tsamp, d), seq_layout, row(norm_mix[0]), conf_w_pw1[0].astype(bf16), row(conf_b_pw1[0]),
        conf_w_dw[0], row(conf_b_dw[0]), row(conf_ln_g[0]), row(conf_ln_b[0]),
        conf_w_pw2[0].astype(bf16), row(conf_b_pw2[0]), row(norm_ffn[0]), wra0, wrb0, tri)
    plan1 = _route_plan(eidx1, lrank1, ts)
    y1 = _moe(_dispatch(h1, plan1), plan1, wgu[0], wd[0])

    tokw = _window_tokens(t, ts, SC_HALO)
    posw1 = jnp.concatenate([plan1["pos"][0][tokw], plan1["pos"][1][tokw]], axis=1)
    posw1 = posw1.reshape(t // ts, 1, 2 * tokw.shape[1])
    gw1 = gates1.T[tokw]
    x2, h2, eidx2, gates2, lrank2 = _sc_layer(
        x1, y1, posw1, gw1, seq_layout, row(norm_mix[1]), sc_w_in[0].astype(bf16), sc_w_dw[0],
        sc_w_out[0].astype(bf16), row(norm_ffn[1]), wra1, wrb1, tri)
    plan2 = _route_plan(eidx2, lrank2, ts)
    y2 = _moe(_dispatch(h2, plan2), plan2, wgu[1], wd[1])

    fg = row(final_norm)
    gates2_t = gates2.T
    out_p = _final(x2, y2, plan2["pos_tiles"], gates2_t, fg, 0, tp // ts)
    out_s = _final(x2, y2, plan2["pos_tiles"], gates2_t, fg, tp // ts, tsamp // ts)
    return out_p.reshape(bp, sp, d), out_s.reshape(bs, ss, d)
```

```python
import functools

import jax
import jax.numpy as jnp
from jax import lax
from jax.experimental import pallas as pl
from jax.experimental.pallas import tpu as pltpu

RMS_EPS = 1e-6
LN_EPS = 1e-5
N_GROUPS = 4
EXPERTS_PER_GROUP = 8
N_EXPERTS = N_GROUPS * EXPERTS_PER_GROUP
ROUTER_ROWS = 40
LANES = 128
SUBLANES = 8
TOKEN_TILE = 512
CONF_HALO = 16
SC_HALO = 8
MOE_BLOCK = 512
CONV_ROWS = 128
ROW_DMA_UNROLL = 8
SC_ISSUE_PIECES = 8
VMEM_LIMIT = 56 * 1024 * 1024

f32 = jnp.float32
bf16 = jnp.bfloat16


def _rms(x, g):
    ms = jnp.mean(x * x, axis=-1, keepdims=True)
    return x * lax.rsqrt(ms + RMS_EPS) * g


def _route(h, wra_ref, wrb_ref, tri_ref, eidx_ref, gate_ref, lrank_ref):
    rows = h.shape[0]
    h_hi = h.astype(bf16)
    h_lo = (h - h_hi.astype(f32)).astype(bf16)
    dn = (((1,), (1,)), ((), ()))
    la = lax.dot_general(wra_ref[...], h_hi, dn, preferred_element_type=f32)
    lb = lax.dot_general(wrb_ref[...], h_lo, dn, preferred_element_type=f32)
    lt = la[:ROUTER_ROWS] + la[ROUTER_ROWS:] + lb
    el = lt[:N_EXPERTS]
    gl = lt[N_EXPERTS:N_EXPERTS + N_GROUPS]
    gm = jnp.max(gl, axis=0, keepdims=True)
    gp = jnp.exp(gl - gm)
    gprob = gp / jnp.sum(gp, axis=0, keepdims=True)
    g_top_p = jnp.max(gprob, axis=0, keepdims=True)
    iota_g = lax.broadcasted_iota(jnp.int32, (N_GROUPS, rows), 0)
    g_top = jnp.min(jnp.where(gprob == g_top_p, iota_g, N_GROUPS), axis=0, keepdims=True)
    elg = jnp.zeros((EXPERTS_PER_GROUP, rows), f32)
    for g in range(N_GROUPS):
        sl = el[g * EXPERTS_PER_GROUP:(g + 1) * EXPERTS_PER_GROUP]
        elg = elg + jnp.where(g_top == g, sl, 0.0)
    em = jnp.max(elg, axis=0, keepdims=True)
    ep = jnp.exp(elg - em)
    eprob = ep / jnp.sum(ep, axis=0, keepdims=True)
    iota_e = lax.broadcasted_iota(jnp.int32, (EXPERTS_PER_GROUP, rows), 0)
    p1 = jnp.max(eprob, axis=0, keepdims=True)
    i1 = jnp.min(jnp.where(eprob == p1, iota_e, EXPERTS_PER_GROUP), axis=0, keepdims=True)
    rest = jnp.where(iota_e == i1, -1.0, eprob)
    p2 = jnp.max(rest, axis=0, keepdims=True)
    i2 = jnp.min(jnp.where(rest == p2, iota_e, EXPERTS_PER_GROUP), axis=0, keepdims=True)
    den = p1 + p2
    e1 = g_top * EXPERTS_PER_GROUP + i1
    e2 = g_top * EXPERTS_PER_GROUP + i2
    eidx_ref[0:1, :] = e1
    eidx_ref[1:2, :] = e2
    gate_ref[0:1, :] = g_top_p * (p1 / den)
    gate_ref[1:2, :] = g_top_p * (p2 / den)

    iota_x = lax.broadcasted_iota(jnp.int32, (N_EXPERTS, rows), 0)
    oh = jnp.concatenate([iota_x == e1, iota_x == e2], axis=0).astype(f32)
    before = jnp.dot(oh.astype(bf16), tri_ref[...], preferred_element_type=f32)
    cnt1 = jnp.sum(oh[:N_EXPERTS], axis=1, keepdims=True)
    r1 = jnp.sum(oh[:N_EXPERTS] * before[:N_EXPERTS], axis=0, keepdims=True)
    r2 = jnp.sum(oh[N_EXPERTS:] * (before[N_EXPERTS:] + cnt1), axis=0, keepdims=True)
    lrank_ref[0:1, :] = r1.astype(jnp.int32)
    lrank_ref[1:2, :] = r2.astype(jnp.int32)


def _seq_edges(i, n_tiles_a, tiles_per_seq_a, tiles_per_seq_b):
    in_a = i < n_tiles_a
    per = jnp.where(in_a, tiles_per_seq_a, tiles_per_seq_b)
    j = jnp.where(in_a, i, i - n_tiles_a) % per
    return j == 0, j == per - 1


def _const_spec(shape):
    nd = len(shape)
    return pl.BlockSpec(shape, lambda i: (0,) * nd, pipeline_mode=pl.Buffered(1))


def _row_tiles(x):
    return x.reshape(x.shape[0], x.shape[1] // LANES, LANES)


def _layer_out(t, d, ts):
    out_specs = [
        pl.BlockSpec((ts, d), lambda i: (i, 0)),
        pl.BlockSpec((ts, d // LANES, LANES), lambda i: (i, 0, 0)),
        pl.BlockSpec((2, ts), lambda i: (0, i)),
        pl.BlockSpec((2, ts), lambda i: (0, i)),
        pl.BlockSpec((2, ts), lambda i: (0, i)),
    ]
    out_shape = [
        jax.ShapeDtypeStruct((t, d), f32),
        jax.ShapeDtypeStruct((t, d // LANES, LANES), bf16),
        jax.ShapeDtypeStruct((2, t), jnp.int32),
        jax.ShapeDtypeStruct((2, t), f32),
        jax.ShapeDtypeStruct((2, t), jnp.int32),
    ]
    return out_specs, out_shape


def _conf_body(aprev_ref, amain_ref, anext_ref, bprev_ref, bmain_ref, bnext_ref,
               nmix_ref, w1_ref, b1_ref, wdw_ref, bdw_ref,
               lng_ref, lnb_ref, w2_ref, b2_ref, nffn_ref, wra_ref, wrb_ref, tri_ref,
               x1_ref, h_ref, eidx_ref, gate_ref, lrank_ref, xn_scr, v_scr, u_scr,
               *, n_tiles_a, tiles_per_seq_a, tiles_per_seq_b):
    ts, d = amain_ref.shape
    halo = CONF_HALO
    w = ts + 2 * halo
    n_col = d // LANES
    taps = wdw_ref.shape[0]
    i = pl.program_id(0)
    is_start, is_end = _seq_edges(i, n_tiles_a, tiles_per_seq_a, tiles_per_seq_b)
    in_a = i < n_tiles_a
    xmain = jnp.where(in_a, amain_ref[...], bmain_ref[...])

    g = nmix_ref[...]
    xn_scr[0:halo, :] = _rms(jnp.where(in_a, aprev_ref[...], bprev_ref[...]), g).astype(bf16)
    xn_scr[halo:halo + ts, :] = _rms(xmain, g).astype(bf16)
    xn_scr[halo + ts:w, :] = _rms(jnp.where(in_a, anext_ref[...], bnext_ref[...]), g).astype(bf16)
    z = jnp.dot(xn_scr[...], w1_ref[...], preferred_element_type=f32) + b1_ref[...]
    v = z[:, :d] * jax.nn.sigmoid(z[:, d:])
    rows = lax.broadcasted_iota(jnp.int32, (w, 1), 0)
    outside = (is_start & (rows < halo)) | (is_end & (rows >= halo + ts))
    v = jnp.where(outside, 0.0, v)
    for j in range(n_col):
        v_scr[j] = v[:, j * LANES:(j + 1) * LANES]

    off = halo - taps // 2
    for j in range(n_col):
        cs = slice(j * LANES, (j + 1) * LANES)
        wj = wdw_ref[:, cs]
        wb = [jnp.broadcast_to(wj[k:k + 1, :], (CONV_ROWS, LANES)) for k in range(taps)]
        bj = bdw_ref[:, cs]
        for c in range(ts // CONV_ROWS):
            acc = v_scr[j, pl.ds(c * CONV_ROWS + off, CONV_ROWS), :] * wb[0]
            for k in range(1, taps):
                acc = acc + v_scr[j, pl.ds(c * CONV_ROWS + off + k, CONV_ROWS), :] * wb[k]
            u_scr[pl.ds(c * CONV_ROWS, CONV_ROWS), cs] = acc + bj

    u = u_scr[...]
    mu = jnp.mean(u, axis=-1, keepdims=True)
    uc = u - mu
    var = jnp.mean(uc * uc, axis=-1, keepdims=True)
    y = uc * lax.rsqrt(var + LN_EPS) * lng_ref[...] + lnb_ref[...]
    s = y * jax.nn.sigmoid(y)
    m = jnp.dot(s.astype(bf16), w2_ref[...], preferred_element_type=f32) + b2_ref[...]
    x1 = jnp.where(in_a, amain_ref[...], bmain_ref[...]) + m
    x1_ref[...] = x1
    h = _rms(x1, nffn_ref[...])
    h_ref[...] = _row_tiles(h.astype(bf16))
    _route(h, wra_ref, wrb_ref, tri_ref, eidx_ref, gate_ref, lrank_ref)


def _halo_specs(n_rows, d, ts, halo, tile_of_step):
    hb = ts // halo
    last = n_rows // halo - 1
    return [
        pl.BlockSpec((halo, d), lambda i: (jnp.maximum(tile_of_step(i) * hb - 1, 0), 0)),
        pl.BlockSpec((ts, d), lambda i: (tile_of_step(i), 0)),
        pl.BlockSpec((halo, d), lambda i: (jnp.minimum((tile_of_step(i) + 1) * hb, last), 0)),
    ]


def _conf_layer(xa, xb, seq_layout, nmix, w1, b1, wdw, bdw, lng, lnb, w2, b2, nffn, wra, wrb, tri):
    d = xa.shape[1]
    t = xa.shape[0] + xb.shape[0]
    ts = TOKEN_TILE
    n_tiles = t // ts
    n_tiles_a, tiles_per_seq_a, tiles_per_seq_b = seq_layout
    consts = (nmix, w1, b1, wdw, bdw, lng, lnb, w2, b2, nffn, wra, wrb, tri)
    in_specs = (
        _halo_specs(xa.shape[0], d, ts, CONF_HALO, lambda i: jnp.minimum(i, n_tiles_a - 1))
        + _halo_specs(xb.shape[0], d, ts, CONF_HALO, lambda i: jnp.maximum(i - n_tiles_a, 0))
        + [_const_spec(a.shape) for a in consts])
    out_specs, out_shape = _layer_out(t, d, ts)
    w = ts + 2 * CONF_HALO
    body = functools.partial(_conf_body, n_tiles_a=n_tiles_a, tiles_per_seq_a=tiles_per_seq_a,
                             tiles_per_seq_b=tiles_per_seq_b)
    return pl.pallas_call(
        body,
        grid=(n_tiles,),
        in_specs=in_specs,
        out_specs=out_specs,
        out_shape=out_shape,
        scratch_shapes=[
            pltpu.VMEM((w, d), bf16),
            pltpu.VMEM((d // LANES, w, LANES), f32),
            pltpu.VMEM((ts, d), f32),
        ],
        compiler_params=pltpu.CompilerParams(
            dimension_semantics=("arbitrary",), vmem_limit_bytes=VMEM_LIMIT),
        name="conf_layer",
    )(xa, xa, xa, xb, xb, xb, *consts)


def _route_plan(eidx, lrank, ts):
    t = eidx.shape[1]
    n_tiles = t // ts
    bm = MOE_BLOCK
    length = ((2 * t + N_EXPERTS * bm + bm - 1) // bm) * bm
    nb = length // bm
    experts = jnp.arange(N_EXPERTS, dtype=jnp.int32)
    tiles = jnp.arange(n_tiles, dtype=jnp.int32)
    eidx3 = eidx.reshape(2, n_tiles, ts)
    onehot = eidx3[..., None] == experts
    cnt = jnp.sum(onehot.astype(jnp.int32), axis=(0, 2))
    earlier = tiles[:, None] > tiles[None, :]
    tile_base = jnp.sum(jnp.where(earlier[:, :, None], cnt[None, :, :], 0), axis=1)
    counts = jnp.sum(cnt, axis=0)
    padded = ((counts + bm - 1) // bm) * bm
    pad_end = jnp.sum(jnp.where(experts[:, None] >= experts[None, :], padded[None, :], 0), axis=1)
    pad_start = pad_end - padded
    base = pad_start[None, :] + tile_base
    pos3 = jnp.sum(jnp.where(onehot, base[None, :, None, :], 0), axis=-1) + lrank.reshape(2, n_tiles, ts)
    pos = pos3.reshape(2, t).astype(jnp.int32)
    blk_start = jnp.arange(nb, dtype=jnp.int32) * bm
    block_e = jnp.minimum(jnp.sum((blk_start[:, None] >= pad_end[None, :]).astype(jnp.int32), axis=1),
                          N_EXPERTS - 1).astype(jnp.int32)
    n_used = (pad_end[-1] // bm).astype(jnp.int32).reshape(1)
    pos_tiles = pos.reshape(2, n_tiles, ts).transpose(1, 0, 2).reshape(n_tiles, 1, 2 * ts)
    tail = n_used[0] + experts
    zero_off = jnp.concatenate([pad_end - bm, tail * bm])
    zero_on = jnp.concatenate([padded > 0, tail < nb])
    zero_off = jnp.where(zero_on, zero_off, -1).astype(jnp.int32)
    return dict(pos=pos, pos_tiles=pos_tiles, block_e=block_e, n_used=n_used,
                zero_off=zero_off, length=length)


def _window_tokens(t, ts, halo):
    n_tiles = t // ts
    w = ts + 2 * halo
    tok = jnp.arange(n_tiles, dtype=jnp.int32)[:, None] * ts - halo + jnp.arange(w, dtype=jnp.int32)[None, :]
    return jnp.clip(tok, 0, t - 1)


def _dispatch_body(zoff_ref, pos_ref, h_ref, hs_hbm, zbuf, sem, zsem):
    ts = h_ref.shape[0]
    bm = zbuf.shape[0]
    n_zero = zoff_ref.shape[0]

    @pl.when(pl.program_id(0) == 0)
    def _():
        zbuf[...] = jnp.zeros_like(zbuf)

        def zero_copy(j):
            off = pl.multiple_of(zoff_ref[j], bm)
            return pltpu.make_async_copy(zbuf, hs_hbm.at[pl.ds(off, bm)], zsem)

        def start(j, carry):
            @pl.when(zoff_ref[j] >= 0)
            def _():
                zero_copy(j).start()
            return carry

        def wait(j, carry):
            @pl.when(zoff_ref[j] >= 0)
            def _():
                zero_copy(j).wait()
            return carry

        lax.fori_loop(0, n_zero, start, 0)
        lax.fori_loop(0, n_zero, wait, 0)

    def body(r8, carry):
        for u in range(ROW_DMA_UNROLL):
            r = r8 * ROW_DMA_UNROLL + u
            for k in range(2):
                p = pos_ref[0, 0, k * ts + r]
                pltpu.make_async_copy(h_ref.at[r], hs_hbm.at[p], sem).start(priority=(2 * u + k) % 2)
        return carry

    lax.fori_loop(0, ts // ROW_DMA_UNROLL, body, 0)
    for _ in range(2):
        pltpu.make_async_copy(hs_hbm.at[pl.ds(0, ts)], hs_hbm.at[pl.ds(ts, ts)], sem).wait()


def _dispatch(h, plan):
    t, n_sub, n_lane = h.shape
    pos_tiles = plan["pos_tiles"]
    n_tiles = pos_tiles.shape[0]
    ts = t // n_tiles
    grid_spec = pltpu.PrefetchScalarGridSpec(
        num_scalar_prefetch=1,
        grid=(n_tiles,),
        in_specs=[
            pl.BlockSpec((1, 1, 2 * ts), lambda i, z: (i, 0, 0), memory_space=pltpu.SMEM),
            pl.BlockSpec((ts, n_sub, n_lane), lambda i, z: (i, 0, 0)),
        ],
        out_specs=pl.BlockSpec(memory_space=pl.ANY),
        scratch_shapes=[pltpu.VMEM((MOE_BLOCK, n_sub, n_lane), h.dtype), pltpu.SemaphoreType.DMA(()),
                        pltpu.SemaphoreType.DMA(())],
    )
    return pl.pallas_call(
        _dispatch_body,
        grid_spec=grid_spec,
        out_shape=jax.ShapeDtypeStruct((plan["length"], n_sub, n_lane), h.dtype),
        compiler_params=pltpu.CompilerParams(dimension_semantics=("arbitrary",)),
        name="dispatch",
    )(plan["zero_off"], pos_tiles, h)


def _moe_body(be_ref, nused_ref, hs_ref, wg_ref, wu_ref, wd_ref, y_ref, wg_scr, wu_scr, wd_scr):
    b = pl.program_id(0)
    used = b < nused_ref[0]

    @pl.when(used & ((b == 0) | (be_ref[b] != be_ref[jnp.maximum(b - 1, 0)])))
    def _():
        wg_scr[...] = wg_ref[0, 0].astype(bf16)
        wu_scr[...] = wu_ref[0, 0].astype(bf16)
        wd_scr[...] = wd_ref[0, 0].astype(bf16)

    @pl.when(used)
    def _():
        bm, n_sub, n_lane = hs_ref.shape
        x = hs_ref[...].reshape(bm, n_sub * n_lane)
        gt = jnp.dot(x, wg_scr[...], preferred_element_type=f32)
        up = jnp.dot(x, wu_scr[...], preferred_element_type=f32)
        hid = gt * jax.nn.sigmoid(gt) * up
        y = jnp.dot(hid.astype(bf16), wd_scr[...], preferred_element_type=f32)
        y_ref[...] = _row_tiles(y)

    @pl.when(jnp.logical_not(used))
    def _():
        y_ref[...] = jnp.zeros_like(y_ref)


def _moe(hs, plan, layer, w_gate, w_up, w_down):
    length, n_sub, n_lane = hs.shape
    bm = MOE_BLOCK
    nb = length // bm
    _, _, d, d_exp = w_gate.shape
    w_in_spec = pl.BlockSpec((1, 1, d, d_exp), lambda b, be, nu: (layer, be[b], 0, 0))
    grid_spec = pltpu.PrefetchScalarGridSpec(
        num_scalar_prefetch=2,
        grid=(nb,),
        in_specs=[
            pl.BlockSpec((bm, n_sub, n_lane), lambda b, be, nu: (jnp.minimum(b, nu[0] - 1), 0, 0)),
            w_in_spec,
            w_in_spec,
            pl.BlockSpec((1, 1, d_exp, d), lambda b, be, nu: (layer, be[b], 0, 0)),
        ],
        out_specs=pl.BlockSpec((bm, n_sub, n_lane), lambda b, be, nu: (b, 0, 0)),
        scratch_shapes=[pltpu.VMEM((d, d_exp), bf16), pltpu.VMEM((d, d_exp), bf16),
                        pltpu.VMEM((d_exp, d), bf16)],
    )
    return pl.pallas_call(
        _moe_body,
        grid_spec=grid_spec,
        out_shape=jax.ShapeDtypeStruct((length, n_sub, n_lane), f32),
        compiler_params=pltpu.CompilerParams(
            dimension_semantics=("arbitrary",), vmem_limit_bytes=VMEM_LIMIT),
        name="moe",
    )(plan["block_e"], plan["n_used"], hs, w_gate, w_up, w_down)


def _combine_gather(i, n_tiles, pos0_ref, posn_ref, y_hbm, gbuf, sem, spread=None):
    _, n_rows, n_sub, n_lane = gbuf.shape

    def issue(idx_ref, slot):
        def body(r8, carry):
            for u in range(ROW_DMA_UNROLL):
                r = r8 * ROW_DMA_UNROLL + u
                p = idx_ref[0, 0, r]
                pltpu.make_async_copy(y_hbm.at[p], gbuf.at[slot, r], sem.at[slot]).start(priority=u % 2)
            return carry
        lax.fori_loop(0, n_rows // ROW_DMA_UNROLL, body, 0)

    @pl.when(i == 0)
    def _():
        issue(pos0_ref, 0)

    slot = i % 2
    nslot = 1 - slot

    def wait(s):
        pltpu.make_async_copy(gbuf.at[s], gbuf.at[s], sem.at[s]).wait()

    if spread is None:
        @pl.when(i + 1 < n_tiles)
        def _():
            issue(posn_ref, nslot)
        issue_next = None
    else:
        piece = -(-n_rows // spread)

        def issue_next(q):
            for r in range(q * piece, min((q + 1) * piece, n_rows)):
                p = posn_ref[0, 0, r]
                pltpu.make_async_copy(y_hbm.at[p], gbuf.at[nslot, r], sem.at[nslot]).start(priority=r % 2)
            if q == spread - 1:
                @pl.when(i == n_tiles - 1)
                def _():
                    wait(nslot)

    wait(slot)
    return gbuf[slot].reshape(n_rows, n_sub * n_lane), issue_next


def _sc_body(pos0_ref, posn_ref, xprev_ref, xmain_ref, xnext_ref, gw_ref, y_hbm, nmix_ref, win_ref,
             wdw_ref, wout_ref, nffn_ref, wra_ref, wrb_ref, tri_ref,
             x2_ref, h_ref, eidx_ref, gate_ref, lrank_ref, gbuf, sem, xin_scr, cu_scr,
             *, n_tiles, n_tiles_a, tiles_per_seq_a, tiles_per_seq_b):
    ts, d = xmain_ref.shape
    halo = SC_HALO
    w = ts + 2 * halo
    n_col = d // LANES
    i = pl.program_id(0)
    is_start, is_end = _seq_edges(i, n_tiles_a, tiles_per_seq_a, tiles_per_seq_b)
    yrows, issue_next = _combine_gather(i, n_tiles, pos0_ref, posn_ref, y_hbm, gbuf, sem,
                                        spread=SC_ISSUE_PIECES)

    xin_scr[0:halo, :] = xprev_ref[...]
    xin_scr[halo:halo + ts, :] = xmain_ref[...]
    xin_scr[halo + ts:w, :] = xnext_ref[...]
    gw = gw_ref[0]
    xc = xin_scr[...] + gw[:, 0:1] * yrows[0:w] + gw[:, 1:2] * yrows[w:2 * w]
    issue_next(0)
    hn = _rms(xc, nmix_ref[...]).astype(bf16)
    issue_next(1)
    z = jnp.dot(hn, win_ref[...], preferred_element_type=f32)
    issue_next(2)
    cu = z[:, d:2 * d] * z[:, 2 * d:]
    rows = lax.broadcasted_iota(jnp.int32, (w, 1), 0)
    outside = (is_start & (rows < halo)) | (is_end & (rows >= halo + ts))
    cu = jnp.where(outside, 0.0, cu)
    for j in range(n_col):
        cu_scr[j] = cu[:, j * LANES:(j + 1) * LANES]
    issue_next(3)
    taps = wdw_ref.shape[0]
    off = halo - taps // 2
    parts = []
    for j in range(n_col):
        cs = slice(j * LANES, (j + 1) * LANES)
        wj = wdw_ref[:, cs]
        acc = cu_scr[j, pl.ds(off, ts), :] * wj[0:1, :]
        for k in range(1, taps):
            acc = acc + cu_scr[j, pl.ds(off + k, ts), :] * wj[k:k + 1, :]
        parts.append(acc)
    conv = jnp.concatenate(parts, axis=1)
    bv = z[halo:halo + ts, :d] * conv
    issue_next(4)
    m = jnp.dot(bv.astype(bf16), wout_ref[...], preferred_element_type=f32)
    issue_next(5)
    x2 = xc[halo:halo + ts, :] + m
    x2_ref[...] = x2
    h = _rms(x2, nffn_ref[...])
    h_ref[...] = _row_tiles(h.astype(bf16))
    issue_next(6)
    _route(h, wra_ref, wrb_ref, tri_ref, eidx_ref, gate_ref, lrank_ref)
    issue_next(7)


def _sc_layer(x1, y, posw, gw, seq_layout, nmix, win, wdw, wout, nffn, wra, wrb, tri):
    t, d = x1.shape
    ts = TOKEN_TILE
    n_tiles = t // ts
    n_tiles_a, tiles_per_seq_a, tiles_per_seq_b = seq_layout
    hb = ts // SC_HALO
    n_hblk = t // SC_HALO
    w = ts + 2 * SC_HALO
    smem = pltpu.SMEM
    consts = (nmix, win, wdw, wout, nffn, wra, wrb, tri)
    in_specs = [
        pl.BlockSpec((1, 1, 2 * w), lambda i: (0, 0, 0), memory_space=smem),
        pl.BlockSpec((1, 1, 2 * w), lambda i: (jnp.minimum(i + 1, n_tiles - 1), 0, 0), memory_space=smem),
        pl.BlockSpec((SC_HALO, d), lambda i: (jnp.maximum(i * hb - 1, 0), 0)),
        pl.BlockSpec((ts, d), lambda i: (i, 0)),
        pl.BlockSpec((SC_HALO, d), lambda i: (jnp.minimum((i + 1) * hb, n_hblk - 1), 0)),
        pl.BlockSpec((1, w, 2), lambda i: (i, 0, 0)),
        pl.BlockSpec(memory_space=pl.ANY),
    ] + [_const_spec(a.shape) for a in consts]
    out_specs, out_shape = _layer_out(t, d, ts)
    body = functools.partial(_sc_body, n_tiles=n_tiles, n_tiles_a=n_tiles_a,
                             tiles_per_seq_a=tiles_per_seq_a, tiles_per_seq_b=tiles_per_seq_b)
    return pl.pallas_call(
        body,
        grid=(n_tiles,),
        in_specs=in_specs,
        out_specs=out_specs,
        out_shape=out_shape,
        scratch_shapes=[
            pltpu.VMEM((2, 2 * w, d // LANES, LANES), f32),
            pltpu.SemaphoreType.DMA((2,)),
            pltpu.VMEM((w, d), f32),
            pltpu.VMEM((d // LANES, w, LANES), f32),
        ],
        compiler_params=pltpu.CompilerParams(
            dimension_semantics=("arbitrary",), vmem_limit_bytes=VMEM_LIMIT),
        name="sc_layer",
    )(posw, posw, x1, x1, x1, gw, y, *consts)


def _final_body(pos0_ref, posn_ref, x_ref, gt_ref, y_hbm, g_ref, o_ref, gbuf, sem, *, n_tiles):
    ts = x_ref.shape[0]
    i = pl.program_id(0)
    yrows, _ = _combine_gather(i, n_tiles, pos0_ref, posn_ref, y_hbm, gbuf, sem)
    gt = gt_ref[...]
    xc = x_ref[...] + gt[:, 0:1] * yrows[0:ts] + gt[:, 1:2] * yrows[ts:2 * ts]
    o_ref[...] = _rms(xc, g_ref[...])


def _final(x2, y, pos_tiles, gates_t, g, tile_off, n_tiles):
    t, d = x2.shape
    ts = TOKEN_TILE
    smem = pltpu.SMEM
    in_specs = [
        pl.BlockSpec((1, 1, 2 * ts), lambda i: (tile_off, 0, 0), memory_space=smem),
        pl.BlockSpec((1, 1, 2 * ts), lambda i: (tile_off + jnp.minimum(i + 1, n_tiles - 1), 0, 0),
                     memory_space=smem),
        pl.BlockSpec((ts, d), lambda i: (tile_off + i, 0)),
        pl.BlockSpec((ts, 2), lambda i: (tile_off + i, 0)),
        pl.BlockSpec(memory_space=pl.ANY),
        _const_spec(g.shape),
    ]
    return pl.pallas_call(
        functools.partial(_final_body, n_tiles=n_tiles),
        grid=(n_tiles,),
        in_specs=in_specs,
        out_specs=pl.BlockSpec((ts, d), lambda i: (i, 0)),
        out_shape=jax.ShapeDtypeStruct((n_tiles * ts, d), f32),
        scratch_shapes=[pltpu.VMEM((2, 2 * ts, d // LANES, LANES), f32),
                        pltpu.SemaphoreType.DMA((2,))],
        compiler_params=pltpu.CompilerParams(
            dimension_semantics=("arbitrary",), vmem_limit_bytes=VMEM_LIMIT),
        name="final_norm",
    )(pos_tiles, pos_tiles, x2, gates_t, y, g)


def _router_weights(w_rg, w_re):
    d = w_rg.shape[0]
    wt = jnp.concatenate([w_re.T, w_rg.T, jnp.zeros((ROUTER_ROWS - N_EXPERTS - N_GROUPS, d), f32)], axis=0)
    hi = wt.astype(bf16)
    lo = (wt - hi.astype(f32)).astype(bf16)
    return jnp.concatenate([hi, lo], axis=0), hi


def kernel(x_prompt, x_sample, norm_mix, conf_w_pw1, conf_b_pw1, conf_w_dw, conf_b_dw, conf_ln_g,
           conf_ln_b, conf_w_pw2, conf_b_pw2, sc_w_in, sc_w_dw, sc_w_out, norm_ffn, router_group,
           router_expert, w_gate, w_up, w_down, final_norm):
    bp, sp, d = x_prompt.shape
    bs, ss, _ = x_sample.shape
    ts = TOKEN_TILE
    assert norm_mix.shape[0] == 2 and sp % ts == 0 and ss % ts == 0 and d % LANES == 0
    tp, tsamp = bp * sp, bs * ss
    t = tp + tsamp
    seq_layout = (tp // ts, sp // ts, ss // ts)

    row = lambda a: a.reshape(1, -1)
    wra0, wrb0 = _router_weights(router_group[0], router_expert[0])
    wra1, wrb1 = _router_weights(router_group[1], router_expert[1])
    ti = jnp.arange(ts, dtype=jnp.int32)
    tri = (ti[:, None] < ti[None, :]).astype(bf16)

    x1, h1, eidx1, gates1, lrank1 = _conf_layer(
        x_prompt.reshape(tp, d), x_sample.reshape(tsamp, d), seq_layout, row(norm_mix[0]), conf_w_pw1[0].astype(bf16), row(conf_b_pw1[0]),
        conf_w_dw[0], row(conf_b_dw[0]), row(conf_ln_g[0]), row(conf_ln_b[0]),
        conf_w_pw2[0].astype(bf16), row(conf_b_pw2[0]), row(norm_ffn[0]), wra0, wrb0, tri)
    plan1 = _route_plan(eidx1, lrank1, ts)
    y1 = _moe(_dispatch(h1, plan1), plan1, 0, w_gate, w_up, w_down)

    tokw = _window_tokens(t, ts, SC_HALO)
    posw1 = jnp.concatenate([plan1["pos"][0][tokw], plan1["pos"][1][tokw]], axis=1)
    posw1 = posw1.reshape(t // ts, 1, 2 * tokw.shape[1])
    gw1 = gates1.T[tokw]
    x2, h2, eidx2, gates2, lrank2 = _sc_layer(
        x1, y1, posw1, gw1, seq_layout, row(norm_mix[1]), sc_w_in[0].astype(bf16), sc_w_dw[0],
        sc_w_out[0].astype(bf16), row(norm_ffn[1]), wra1, wrb1, tri)
    plan2 = _route_plan(eidx2, lrank2, ts)
    y2 = _moe(_dispatch(h2, plan2), plan2, 1, w_gate, w_up, w_down)

    fg = row(final_norm)
    gates2_t = gates2.T
    out_p = _final(x2, y2, plan2["pos_tiles"], gates2_t, fg, 0, tp // ts)
    out_s = _final(x2, y2, plan2["pos_tiles"], gates2_t, fg, tp // ts, tsamp // ts)
    return out_p.reshape(bp, sp, d), out_s.reshape(bs, ss, d)
```

```python
import functools

import jax
import jax.numpy as jnp
from jax import lax
from jax.experimental import pallas as pl
from jax.experimental.pallas import tpu as pltpu

RMS_EPS = 1e-6
LN_EPS = 1e-5
N_GROUPS = 4
EXPERTS_PER_GROUP = 8
N_EXPERTS = N_GROUPS * EXPERTS_PER_GROUP
ROUTER_ROWS = 40
LANES = 128
SUBLANES = 8
TOKEN_TILE = 512
CONF_HALO = 16
SC_HALO = 8
MOE_BLOCK = 512
CONV_ROWS = 128
CONF_PARTS = 1
TOKEN_DIGIT_BITS = 5
ROW_DMA_UNROLL = 8
SC_ISSUE_PIECES = 8
MOE_ISSUE_PIECES = 4
VMEM_LIMIT = 56 * 1024 * 1024

f32 = jnp.float32
bf16 = jnp.bfloat16


def _rms(x, g):
    ms = jnp.mean(x * x, axis=-1, keepdims=True)
    return x * lax.rsqrt(ms + RMS_EPS) * g


def _route(h, tile, wra_ref, wrb_ref, tri_ref, eidx_ref, gate_ref, lrank_ref, loc_ref):
    rows = h.shape[0]
    h_hi = h.astype(bf16)
    h_lo = (h - h_hi.astype(f32)).astype(bf16)
    dn = (((1,), (1,)), ((), ()))
    la = lax.dot_general(wra_ref[...], h_hi, dn, preferred_element_type=f32)
    lb = lax.dot_general(wrb_ref[...], h_lo, dn, preferred_element_type=f32)
    lt = la[:ROUTER_ROWS] + la[ROUTER_ROWS:] + lb
    el = lt[:N_EXPERTS]
    gl = lt[N_EXPERTS:N_EXPERTS + N_GROUPS]
    gm = jnp.max(gl, axis=0, keepdims=True)
    gp = jnp.exp(gl - gm)
    gprob = gp / jnp.sum(gp, axis=0, keepdims=True)
    g_top_p = jnp.max(gprob, axis=0, keepdims=True)
    iota_g = lax.broadcasted_iota(jnp.int32, (N_GROUPS, rows), 0)
    g_top = jnp.min(jnp.where(gprob == g_top_p, iota_g, N_GROUPS), axis=0, keepdims=True)
    elg = jnp.zeros((EXPERTS_PER_GROUP, rows), f32)
    for g in range(N_GROUPS):
        sl = el[g * EXPERTS_PER_GROUP:(g + 1) * EXPERTS_PER_GROUP]
        elg = elg + jnp.where(g_top == g, sl, 0.0)
    em = jnp.max(elg, axis=0, keepdims=True)
    ep = jnp.exp(elg - em)
    eprob = ep / jnp.sum(ep, axis=0, keepdims=True)
    iota_e = lax.broadcasted_iota(jnp.int32, (EXPERTS_PER_GROUP, rows), 0)
    p1 = jnp.max(eprob, axis=0, keepdims=True)
    i1 = jnp.min(jnp.where(eprob == p1, iota_e, EXPERTS_PER_GROUP), axis=0, keepdims=True)
    rest = jnp.where(iota_e == i1, -1.0, eprob)
    p2 = jnp.max(rest, axis=0, keepdims=True)
    i2 = jnp.min(jnp.where(rest == p2, iota_e, EXPERTS_PER_GROUP), axis=0, keepdims=True)
    den = p1 + p2
    e1 = g_top * EXPERTS_PER_GROUP + i1
    e2 = g_top * EXPERTS_PER_GROUP + i2
    eidx_ref[0:1, :] = e1
    eidx_ref[1:2, :] = e2
    gate_ref[0:1, :] = g_top_p * (p1 / den)
    gate_ref[1:2, :] = g_top_p * (p2 / den)

    iota_x = lax.broadcasted_iota(jnp.int32, (N_EXPERTS, rows), 0)
    oh = jnp.concatenate([iota_x == e1, iota_x == e2], axis=0).astype(f32)
    before = jnp.dot(oh.astype(bf16), tri_ref[...], preferred_element_type=f32)
    cnt1 = jnp.sum(oh[:N_EXPERTS], axis=1, keepdims=True)
    r1 = jnp.sum(oh[:N_EXPERTS] * before[:N_EXPERTS], axis=0, keepdims=True)
    r2 = jnp.sum(oh[N_EXPERTS:] * (before[N_EXPERTS:] + cnt1), axis=0, keepdims=True)
    lrank_ref[0:1, :] = r1.astype(jnp.int32)
    lrank_ref[1:2, :] = r2.astype(jnp.int32)

    cnt_all = cnt1 + jnp.sum(oh[N_EXPERTS:], axis=1, keepdims=True)
    s1 = (jnp.sum(jnp.where(iota_x < e1, cnt_all, 0.0), axis=0, keepdims=True) + r1).astype(jnp.int32)
    s2 = (jnp.sum(jnp.where(iota_x < e2, cnt_all, 0.0), axis=0, keepdims=True) + r2).astype(jnp.int32)
    iota_s = lax.broadcasted_iota(jnp.int32, (2 * rows, rows), 0)
    pmat = jnp.where((iota_s == s1) | (iota_s == s2), 1.0, 0.0).astype(bf16)
    tok = lax.broadcasted_iota(jnp.int32, (SUBLANES, rows), 1)
    digit = lax.broadcasted_iota(jnp.int32, (SUBLANES, rows), 0)
    digits = jnp.where(digit == 0, tok >> TOKEN_DIGIT_BITS,
                       jnp.where(digit == 1, tok & ((1 << TOKEN_DIGIT_BITS) - 1), 0))
    listed = lax.dot_general(digits.astype(f32).astype(bf16), pmat, dn, preferred_element_type=f32)
    loc = (1 << TOKEN_DIGIT_BITS) * listed[0:1] + listed[1:2]
    loc_ref[...] = loc.astype(jnp.int32) + tile * rows


def _seq_edges(i, n_tiles_a, tiles_per_seq_a, tiles_per_seq_b):
    in_a = i < n_tiles_a
    per = jnp.where(in_a, tiles_per_seq_a, tiles_per_seq_b)
    j = jnp.where(in_a, i, i - n_tiles_a) % per
    return j == 0, j == per - 1


def _const_spec(shape):
    nd = len(shape)
    return pl.BlockSpec(shape, lambda i: (0,) * nd, pipeline_mode=pl.Buffered(1))


def _row_tiles(x):
    return x.reshape(x.shape[0], x.shape[1] // LANES, LANES)


def _layer_out(t, d, ts):
    out_specs = [
        pl.BlockSpec((ts, d), lambda i: (i, 0)),
        pl.BlockSpec((ts, d // LANES, LANES), lambda i: (i, 0, 0)),
        pl.BlockSpec((2, ts), lambda i: (0, i)),
        pl.BlockSpec((2, ts), lambda i: (0, i)),
        pl.BlockSpec((2, ts), lambda i: (0, i)),
        pl.BlockSpec((1, 2 * ts), lambda i: (0, i)),
    ]
    out_shape = [
        jax.ShapeDtypeStruct((t, d), f32),
        jax.ShapeDtypeStruct((t, d // LANES, LANES), bf16),
        jax.ShapeDtypeStruct((2, t), jnp.int32),
        jax.ShapeDtypeStruct((2, t), f32),
        jax.ShapeDtypeStruct((2, t), jnp.int32),
        jax.ShapeDtypeStruct((1, 2 * t), jnp.int32),
    ]
    return out_specs, out_shape


def _conf_body(aprev_ref, amain_ref, anext_ref, bprev_ref, bmain_ref, bnext_ref,
               nmix_ref, w1_ref, b1_ref, wdw_ref, bdw_ref,
               lng_ref, lnb_ref, w2_ref, b2_ref, nffn_ref, wra_ref, wrb_ref, tri_ref,
               x1_ref, h_ref, eidx_ref, gate_ref, lrank_ref, loc_ref, xn_scr, v_scr, u_scr,
               *, n_tiles_a, tiles_per_seq_a, tiles_per_seq_b):
    ts, d = amain_ref.shape
    halo = CONF_HALO
    w = ts + 2 * halo
    n_col = d // LANES
    taps = wdw_ref.shape[0]
    i = pl.program_id(0)
    is_start, is_end = _seq_edges(i, n_tiles_a, tiles_per_seq_a, tiles_per_seq_b)
    in_a = i < n_tiles_a
    xmain = jnp.where(in_a, amain_ref[...], bmain_ref[...])

    g = nmix_ref[...]
    xn_scr[0:halo, :] = _rms(jnp.where(in_a, aprev_ref[...], bprev_ref[...]), g).astype(bf16)
    xn_scr[halo:halo + ts, :] = _rms(xmain, g).astype(bf16)
    xn_scr[halo + ts:w, :] = _rms(jnp.where(in_a, anext_ref[...], bnext_ref[...]), g).astype(bf16)
    n_part = v_scr.shape[0]
    part = ts // n_part
    pw = part + 2 * halo
    off = halo - taps // 2
    hs = []
    for s in range(n_part):
        r0 = s * part
        z = jnp.dot(xn_scr[r0:r0 + pw, :], w1_ref[...], preferred_element_type=f32) + b1_ref[...]
        v = z[:, :d] * jax.nn.sigmoid(z[:, d:])
        rows = r0 + lax.broadcasted_iota(jnp.int32, (pw, 1), 0)
        outside = (is_start & (rows < halo)) | (is_end & (rows >= halo + ts))
        v = jnp.where(outside, 0.0, v)
        for j in range(n_col):
            v_scr[s, j] = v[:, j * LANES:(j + 1) * LANES]

        for j in range(n_col):
            cs = slice(j * LANES, (j + 1) * LANES)
            wj = wdw_ref[:, cs]
            wb = [jnp.broadcast_to(wj[k:k + 1, :], (CONV_ROWS, LANES)) for k in range(taps)]
            bj = bdw_ref[:, cs]
            for c in range(part // CONV_ROWS):
                acc = v_scr[s, j, pl.ds(c * CONV_ROWS + off, CONV_ROWS), :] * wb[0]
                for k in range(1, taps):
                    acc = acc + v_scr[s, j, pl.ds(c * CONV_ROWS + off + k, CONV_ROWS), :] * wb[k]
                u_scr[pl.ds(r0 + c * CONV_ROWS, CONV_ROWS), cs] = acc + bj

        u = u_scr[r0:r0 + part, :]
        mu = jnp.mean(u, axis=-1, keepdims=True)
        uc = u - mu
        var = jnp.mean(uc * uc, axis=-1, keepdims=True)
        y = uc * lax.rsqrt(var + LN_EPS) * lng_ref[...] + lnb_ref[...]
        act = y * jax.nn.sigmoid(y)
        m = jnp.dot(act.astype(bf16), w2_ref[...], preferred_element_type=f32) + b2_ref[...]
        x1 = jnp.where(in_a, amain_ref[r0:r0 + part, :], bmain_ref[r0:r0 + part, :]) + m
        x1_ref[r0:r0 + part, :] = x1
        h = _rms(x1, nffn_ref[...])
        h_ref[r0:r0 + part] = _row_tiles(h.astype(bf16))
        hs.append(h)
    _route(jnp.concatenate(hs, axis=0), i, wra_ref, wrb_ref, tri_ref, eidx_ref, gate_ref, lrank_ref,
           loc_ref)


def _halo_specs(n_rows, d, ts, halo, tile_of_step):
    hb = ts // halo
    last = n_rows // halo - 1
    return [
        pl.BlockSpec((halo, d), lambda i: (jnp.maximum(tile_of_step(i) * hb - 1, 0), 0)),
        pl.BlockSpec((ts, d), lambda i: (tile_of_step(i), 0)),
        pl.BlockSpec((halo, d), lambda i: (jnp.minimum((tile_of_step(i) + 1) * hb, last), 0)),
    ]


def _conf_layer(xa, xb, seq_layout, nmix, w1, b1, wdw, bdw, lng, lnb, w2, b2, nffn, wra, wrb, tri):
    d = xa.shape[1]
    t = xa.shape[0] + xb.shape[0]
    ts = TOKEN_TILE
    n_tiles = t // ts
    n_tiles_a, tiles_per_seq_a, tiles_per_seq_b = seq_layout
    consts = (nmix, w1, b1, wdw, bdw, lng, lnb, w2, b2, nffn, wra, wrb, tri)
    in_specs = (
        _halo_specs(xa.shape[0], d, ts, CONF_HALO, lambda i: jnp.minimum(i, n_tiles_a - 1))
        + _halo_specs(xb.shape[0], d, ts, CONF_HALO, lambda i: jnp.maximum(i - n_tiles_a, 0))
        + [_const_spec(a.shape) for a in consts])
    out_specs, out_shape = _layer_out(t, d, ts)
    w = ts + 2 * CONF_HALO
    body = functools.partial(_conf_body, n_tiles_a=n_tiles_a, tiles_per_seq_a=tiles_per_seq_a,
                             tiles_per_seq_b=tiles_per_seq_b)
    return pl.pallas_call(
        body,
        grid=(n_tiles,),
        in_specs=in_specs,
        out_specs=out_specs,
        out_shape=out_shape,
        scratch_shapes=[
            pltpu.VMEM((w, d), bf16),
            pltpu.VMEM((CONF_PARTS, d // LANES, ts // CONF_PARTS + 2 * CONF_HALO, LANES), f32),
            pltpu.VMEM((ts, d), f32),
        ],
        compiler_params=pltpu.CompilerParams(
            dimension_semantics=("arbitrary",), vmem_limit_bytes=VMEM_LIMIT),
        name="conf_layer",
    )(xa, xa, xa, xb, xb, xb, *consts)


def _route_plan(eidx, lrank, loc, ts):
    t = eidx.shape[1]
    n_tiles = t // ts
    bm = MOE_BLOCK
    length = ((2 * t + N_EXPERTS * bm + bm - 1) // bm) * bm
    nb = length // bm
    experts = jnp.arange(N_EXPERTS, dtype=jnp.int32)
    tiles = jnp.arange(n_tiles, dtype=jnp.int32)
    eidx3 = eidx.reshape(2, n_tiles, ts)
    onehot = eidx3[..., None] == experts
    cnt = jnp.sum(onehot.astype(jnp.int32), axis=(0, 2))
    earlier = tiles[:, None] > tiles[None, :]
    tile_base = jnp.sum(jnp.where(earlier[:, :, None], cnt[None, :, :], 0), axis=1)
    counts = jnp.sum(cnt, axis=0)
    padded = ((counts + bm - 1) // bm) * bm
    pad_end = jnp.sum(jnp.where(experts[:, None] >= experts[None, :], padded[None, :], 0), axis=1)
    pad_start = pad_end - padded
    base = pad_start[None, :] + tile_base
    pos3 = jnp.sum(jnp.where(onehot, base[None, :, None, :], 0), axis=-1) + lrank.reshape(2, n_tiles, ts)
    pos = pos3.reshape(2, t).astype(jnp.int32)
    blk_start = jnp.arange(nb, dtype=jnp.int32) * bm
    block_e = jnp.minimum(jnp.sum((blk_start[:, None] >= pad_end[None, :]).astype(jnp.int32), axis=1),
                          N_EXPERTS - 1).astype(jnp.int32)
    n_used = (pad_end[-1] // bm).astype(jnp.int32).reshape(1)
    pos_tiles = pos.reshape(2, n_tiles, ts).transpose(1, 0, 2).reshape(n_tiles, 1, 2 * ts)

    lstart = jnp.sum(jnp.where(experts[:, None] > experts[None, :], cnt[:, None, :], 0), axis=2)
    tile_end = tile_base + cnt
    delta = tiles[:, None] * (2 * ts) + lstart - tile_base
    step = delta[1:] - delta[:-1]
    mine = block_e[:, None] == experts[None, :]
    pick = lambda table: jnp.sum(jnp.where(mine, table[None, :], 0), axis=1)
    pick_rows = lambda table: jnp.sum(jnp.where(mine[:, None, :], table[None, :, :], 0), axis=2)
    rank = (blk_start - pick(pad_start))[:, None] + jnp.arange(bm, dtype=jnp.int32)[None, :]
    passed = pick_rows(tile_end[:-1])[:, None, :] <= rank[:, :, None]
    src = rank + pick(delta[0])[:, None] + jnp.sum(jnp.where(passed, pick_rows(step)[:, None, :], 0), axis=2)
    valid = rank < pick(counts)[:, None]
    tok = jnp.where(valid, loc.reshape(-1)[jnp.where(valid, src, 0)], 0)
    return dict(pos=pos, pos_tiles=pos_tiles, block_e=block_e, n_used=n_used,
                tok=tok.reshape(nb, 1, bm).astype(jnp.int32), length=length)


def _window_tokens(t, ts, halo):
    n_tiles = t // ts
    w = ts + 2 * halo
    tok = jnp.arange(n_tiles, dtype=jnp.int32)[:, None] * ts - halo + jnp.arange(w, dtype=jnp.int32)[None, :]
    return jnp.clip(tok, 0, t - 1)


def _moe_body(be_ref, nused_ref, tok0_ref, tokn_ref, h_hbm, wg_ref, wu_ref, wd_ref, y_ref,
              buf, sem, wg_scr, wu_scr, wd_scr):
    b = pl.program_id(0)
    n_used = nused_ref[0]
    used = b < n_used
    _, bm, n_sub, n_lane = buf.shape

    def wait(s):
        pltpu.make_async_copy(buf.at[s], buf.at[s], sem.at[s]).wait()

    @pl.when(b == 0)
    def _():
        def body(r8, carry):
            for u in range(ROW_DMA_UNROLL):
                r = r8 * ROW_DMA_UNROLL + u
                pltpu.make_async_copy(h_hbm.at[tok0_ref[0, 0, r]], buf.at[0, r], sem.at[0]).start(priority=u % 2)
            return carry
        lax.fori_loop(0, bm // ROW_DMA_UNROLL, body, 0)

    @pl.when(used & ((b == 0) | (be_ref[b] != be_ref[jnp.maximum(b - 1, 0)])))
    def _():
        wg_scr[...] = wg_ref[0, 0].astype(bf16)
        wu_scr[...] = wu_ref[0, 0].astype(bf16)
        wd_scr[...] = wd_ref[0, 0].astype(bf16)

    @pl.when(used)
    def _():
        slot = b % 2
        nslot = 1 - slot
        piece = bm // MOE_ISSUE_PIECES

        def issue_next(q):
            for r in range(q * piece, (q + 1) * piece):
                pltpu.make_async_copy(h_hbm.at[tokn_ref[0, 0, r]], buf.at[nslot, r],
                                      sem.at[nslot]).start(priority=r % 2)

        wait(slot)
        x = buf[slot].reshape(bm, n_sub * n_lane)
        issue_next(0)
        gt = jnp.dot(x, wg_scr[...], preferred_element_type=f32)
        issue_next(1)
        up = jnp.dot(x, wu_scr[...], preferred_element_type=f32)
        issue_next(2)
        hid = gt * jax.nn.sigmoid(gt) * up
        y = jnp.dot(hid.astype(bf16), wd_scr[...], preferred_element_type=f32)
        issue_next(3)
        y_ref[...] = _row_tiles(y)

        @pl.when(b == n_used - 1)
        def _():
            wait(nslot)

    @pl.when(jnp.logical_not(used))
    def _():
        y_ref[...] = jnp.zeros_like(y_ref)


def _moe(h, plan, layer, w_gate, w_up, w_down):
    _, n_sub, n_lane = h.shape
    nb, _, bm = plan["tok"].shape
    _, _, d, d_exp = w_gate.shape
    w_in_spec = pl.BlockSpec((1, 1, d, d_exp), lambda b, be, nu: (layer, be[b], 0, 0))
    smem = pltpu.SMEM
    grid_spec = pltpu.PrefetchScalarGridSpec(
        num_scalar_prefetch=2,
        grid=(nb,),
        in_specs=[
            pl.BlockSpec((1, 1, bm), lambda b, be, nu: (0, 0, 0), memory_space=smem),
            pl.BlockSpec((1, 1, bm), lambda b, be, nu: (jnp.minimum(b + 1, nu[0] - 1), 0, 0),
                         memory_space=smem),
            pl.BlockSpec(memory_space=pl.ANY),
            w_in_spec,
            w_in_spec,
            pl.BlockSpec((1, 1, d_exp, d), lambda b, be, nu: (layer, be[b], 0, 0)),
        ],
        out_specs=pl.BlockSpec((bm, n_sub, n_lane), lambda b, be, nu: (b, 0, 0)),
        scratch_shapes=[pltpu.VMEM((2, bm, n_sub, n_lane), h.dtype), pltpu.SemaphoreType.DMA((2,)),
                        pltpu.VMEM((d, d_exp), bf16), pltpu.VMEM((d, d_exp), bf16),
                        pltpu.VMEM((d_exp, d), bf16)],
    )
    return pl.pallas_call(
        _moe_body,
        grid_spec=grid_spec,
        out_shape=jax.ShapeDtypeStruct((nb * bm, n_sub, n_lane), f32),
        compiler_params=pltpu.CompilerParams(
            dimension_semantics=("arbitrary",), vmem_limit_bytes=VMEM_LIMIT),
        name="moe",
    )(plan["block_e"], plan["n_used"], plan["tok"], plan["tok"], h, w_gate, w_up, w_down)


def _combine_gather(i, n_tiles, pos0_ref, posn_ref, y_hbm, gbuf, sem, spread=None):
    _, n_rows, n_sub, n_lane = gbuf.shape

    def issue(idx_ref, slot):
        def body(r8, carry):
            for u in range(ROW_DMA_UNROLL):
                r = r8 * ROW_DMA_UNROLL + u
                p = idx_ref[0, 0, r]
                pltpu.make_async_copy(y_hbm.at[p], gbuf.at[slot, r], sem.at[slot]).start(priority=u % 2)
            return carry
        lax.fori_loop(0, n_rows // ROW_DMA_UNROLL, body, 0)

    @pl.when(i == 0)
    def _():
        issue(pos0_ref, 0)

    slot = i % 2
    nslot = 1 - slot

    def wait(s):
        pltpu.make_async_copy(gbuf.at[s], gbuf.at[s], sem.at[s]).wait()

    if spread is None:
        @pl.when(i + 1 < n_tiles)
        def _():
            issue(posn_ref, nslot)
        issue_next = None
    else:
        piece = -(-n_rows // spread)

        def issue_next(q):
            for r in range(q * piece, min((q + 1) * piece, n_rows)):
                p = posn_ref[0, 0, r]
                pltpu.make_async_copy(y_hbm.at[p], gbuf.at[nslot, r], sem.at[nslot]).start(priority=r % 2)
            if q == spread - 1:
                @pl.when(i == n_tiles - 1)
                def _():
                    wait(nslot)

    wait(slot)
    return gbuf[slot].reshape(n_rows, n_sub * n_lane), issue_next


def _sc_body(pos0_ref, posn_ref, xprev_ref, xmain_ref, xnext_ref, gw_ref, y_hbm, nmix_ref, win_ref,
             wdw_ref, wout_ref, nffn_ref, wra_ref, wrb_ref, tri_ref,
             x2_ref, h_ref, eidx_ref, gate_ref, lrank_ref, loc_ref, gbuf, sem, xin_scr, cu_scr,
             *, n_tiles, n_tiles_a, tiles_per_seq_a, tiles_per_seq_b):
    ts, d = xmain_ref.shape
    halo = SC_HALO
    w = ts + 2 * halo
    n_col = d // LANES
    i = pl.program_id(0)
    is_start, is_end = _seq_edges(i, n_tiles_a, tiles_per_seq_a, tiles_per_seq_b)
    yrows, issue_next = _combine_gather(i, n_tiles, pos0_ref, posn_ref, y_hbm, gbuf, sem,
                                        spread=SC_ISSUE_PIECES)

    xin_scr[0:halo, :] = xprev_ref[...]
    xin_scr[halo:halo + ts, :] = xmain_ref[...]
    xin_scr[halo + ts:w, :] = xnext_ref[...]
    gw = gw_ref[0]
    xc = xin_scr[...] + gw[:, 0:1] * yrows[0:w] + gw[:, 1:2] * yrows[w:2 * w]
    issue_next(0)
    hn = _rms(xc, nmix_ref[...]).astype(bf16)
    issue_next(1)
    z = jnp.dot(hn, win_ref[...], preferred_element_type=f32)
    issue_next(2)
    cu = z[:, d:2 * d] * z[:, 2 * d:]
    rows = lax.broadcasted_iota(jnp.int32, (w, 1), 0)
    outside = (is_start & (rows < halo)) | (is_end & (rows >= halo + ts))
    cu = jnp.where(outside, 0.0, cu)
    for j in range(n_col):
        cu_scr[j] = cu[:, j * LANES:(j + 1) * LANES]
    issue_next(3)
    taps = wdw_ref.shape[0]
    off = halo - taps // 2
    parts = []
    for j in range(n_col):
        cs = slice(j * LANES, (j + 1) * LANES)
        wj = wdw_ref[:, cs]
        acc = cu_scr[j, pl.ds(off, ts), :] * wj[0:1, :]
        for k in range(1, taps):
            acc = acc + cu_scr[j, pl.ds(off + k, ts), :] * wj[k:k + 1, :]
        parts.append(acc)
    conv = jnp.concatenate(parts, axis=1)
    bv = z[halo:halo + ts, :d] * conv
    issue_next(4)
    m = jnp.dot(bv.astype(bf16), wout_ref[...], preferred_element_type=f32)
    issue_next(5)
    x2 = xc[halo:halo + ts, :] + m
    x2_ref[...] = x2
    h = _rms(x2, nffn_ref[...])
    h_ref[...] = _row_tiles(h.astype(bf16))
    issue_next(6)
    _route(h, i, wra_ref, wrb_ref, tri_ref, eidx_ref, gate_ref, lrank_ref, loc_ref)
    issue_next(7)


def _sc_layer(x1, y, posw, gw, seq_layout, nmix, win, wdw, wout, nffn, wra, wrb, tri):
    t, d = x1.shape
    ts = TOKEN_TILE
    n_tiles = t // ts
    n_tiles_a, tiles_per_seq_a, tiles_per_seq_b = seq_layout
    hb = ts // SC_HALO
    n_hblk = t // SC_HALO
    w = ts + 2 * SC_HALO
    smem = pltpu.SMEM
    consts = (nmix, win, wdw, wout, nffn, wra, wrb, tri)
    in_specs = [
        pl.BlockSpec((1, 1, 2 * w), lambda i: (0, 0, 0), memory_space=smem),
        pl.BlockSpec((1, 1, 2 * w), lambda i: (jnp.minimum(i + 1, n_tiles - 1), 0, 0), memory_space=smem),
        pl.BlockSpec((SC_HALO, d), lambda i: (jnp.maximum(i * hb - 1, 0), 0)),
        pl.BlockSpec((ts, d), lambda i: (i, 0)),
        pl.BlockSpec((SC_HALO, d), lambda i: (jnp.minimum((i + 1) * hb, n_hblk - 1), 0)),
        pl.BlockSpec((1, w, 2), lambda i: (i, 0, 0)),
        pl.BlockSpec(memory_space=pl.ANY),
    ] + [_const_spec(a.shape) for a in consts]
    out_specs, out_shape = _layer_out(t, d, ts)
    body = functools.partial(_sc_body, n_tiles=n_tiles, n_tiles_a=n_tiles_a,
                             tiles_per_seq_a=tiles_per_seq_a, tiles_per_seq_b=tiles_per_seq_b)
    return pl.pallas_call(
        body,
        grid=(n_tiles,),
        in_specs=in_specs,
        out_specs=out_specs,
        out_shape=out_shape,
        scratch_shapes=[
            pltpu.VMEM((2, 2 * w, d // LANES, LANES), f32),
            pltpu.SemaphoreType.DMA((2,)),
            pltpu.VMEM((w, d), f32),
            pltpu.VMEM((d // LANES, w, LANES), f32),
        ],
        compiler_params=pltpu.CompilerParams(
            dimension_semantics=("arbitrary",), vmem_limit_bytes=VMEM_LIMIT),
        name="sc_layer",
    )(posw, posw, x1, x1, x1, gw, y, *consts)


def _final_body(pos0_ref, posn_ref, x_ref, gt_ref, y_hbm, g_ref, o_ref, gbuf, sem, *, n_tiles):
    ts = x_ref.shape[0]
    i = pl.program_id(0)
    yrows, _ = _combine_gather(i, n_tiles, pos0_ref, posn_ref, y_hbm, gbuf, sem)
    gt = gt_ref[...]
    xc = x_ref[...] + gt[:, 0:1] * yrows[0:ts] + gt[:, 1:2] * yrows[ts:2 * ts]
    o_ref[...] = _rms(xc, g_ref[...])


def _final(x2, y, pos_tiles, gates_t, g, tile_off, n_tiles):
    t, d = x2.shape
    ts = TOKEN_TILE
    smem = pltpu.SMEM
    in_specs = [
        pl.BlockSpec((1, 1, 2 * ts), lambda i: (tile_off, 0, 0), memory_space=smem),
        pl.BlockSpec((1, 1, 2 * ts), lambda i: (tile_off + jnp.minimum(i + 1, n_tiles - 1), 0, 0),
                     memory_space=smem),
        pl.BlockSpec((ts, d), lambda i: (tile_off + i, 0)),
        pl.BlockSpec((ts, 2), lambda i: (tile_off + i, 0)),
        pl.BlockSpec(memory_space=pl.ANY),
        _const_spec(g.shape),
    ]
    return pl.pallas_call(
        functools.partial(_final_body, n_tiles=n_tiles),
        grid=(n_tiles,),
        in_specs=in_specs,
        out_specs=pl.BlockSpec((ts, d), lambda i: (i, 0)),
        out_shape=jax.ShapeDtypeStruct((n_tiles * ts, d), f32),
        scratch_shapes=[pltpu.VMEM((2, 2 * ts, d // LANES, LANES), f32),
                        pltpu.SemaphoreType.DMA((2,))],
        compiler_params=pltpu.CompilerParams(
            dimension_semantics=("arbitrary",), vmem_limit_bytes=VMEM_LIMIT),
        name="final_norm",
    )(pos_tiles, pos_tiles, x2, gates_t, y, g)


def _router_weights(w_rg, w_re):
    d = w_rg.shape[0]
    wt = jnp.concatenate([w_re.T, w_rg.T, jnp.zeros((ROUTER_ROWS - N_EXPERTS - N_GROUPS, d), f32)], axis=0)
    hi = wt.astype(bf16)
    lo = (wt - hi.astype(f32)).astype(bf16)
    return jnp.concatenate([hi, lo], axis=0), hi


def kernel(x_prompt, x_sample, norm_mix, conf_w_pw1, conf_b_pw1, conf_w_dw, conf_b_dw, conf_ln_g,
           conf_ln_b, conf_w_pw2, conf_b_pw2, sc_w_in, sc_w_dw, sc_w_out, norm_ffn, router_group,
           router_expert, w_gate, w_up, w_down, final_norm):
    bp, sp, d = x_prompt.shape
    bs, ss, _ = x_sample.shape
    ts = TOKEN_TILE
    assert norm_mix.shape[0] == 2 and sp % ts == 0 and ss % ts == 0 and d % LANES == 0
    tp, tsamp = bp * sp, bs * ss
    t = tp + tsamp
    seq_layout = (tp // ts, sp // ts, ss // ts)

    row = lambda a: a.reshape(1, -1)
    wra0, wrb0 = _router_weights(router_group[0], router_expert[0])
    wra1, wrb1 = _router_weights(router_group[1], router_expert[1])
    ti = jnp.arange(ts, dtype=jnp.int32)
    tri = (ti[:, None] < ti[None, :]).astype(bf16)

    x1, h1, eidx1, gates1, lrank1, loc1 = _conf_layer(
        x_prompt.reshape(tp, d), x_sample.reshape(tsamp, d), seq_layout, row(norm_mix[0]), conf_w_pw1[0].astype(bf16), row(conf_b_pw1[0]),
        conf_w_dw[0], row(conf_b_dw[0]), row(conf_ln_g[0]), row(conf_ln_b[0]),
        conf_w_pw2[0].astype(bf16), row(conf_b_pw2[0]), row(norm_ffn[0]), wra0, wrb0, tri)
    plan1 = _route_plan(eidx1, lrank1, loc1, ts)
    y1 = _moe(h1, plan1, 0, w_gate, w_up, w_down)

    tokw = _window_tokens(t, ts, SC_HALO)
    posw1 = jnp.concatenate([plan1["pos"][0][tokw], plan1["pos"][1][tokw]], axis=1)
    posw1 = posw1.reshape(t // ts, 1, 2 * tokw.shape[1])
    gw1 = gates1.T[tokw]
    x2, h2, eidx2, gates2, lrank2, loc2 = _sc_layer(
        x1, y1, posw1, gw1, seq_layout, row(norm_mix[1]), sc_w_in[0].astype(bf16), sc_w_dw[0],
        sc_w_out[0].astype(bf16), row(norm_ffn[1]), wra1, wrb1, tri)
    plan2 = _route_plan(eidx2, lrank2, loc2, ts)
    y2 = _moe(h2, plan2, 1, w_gate, w_up, w_down)

    fg = row(final_norm)
    gates2_t = gates2.T
    out_p = _final(x2, y2, plan2["pos_tiles"], gates2_t, fg, 0, tp // ts)
    out_s = _final(x2, y2, plan2["pos_tiles"], gates2_t, fg, tp // ts, tsamp // ts)
    return out_p.reshape(bp, sp, d), out_s.reshape(bs, ss, d)
```

```python
import functools

import jax
import jax.numpy as jnp
from jax import lax
from jax.experimental import pallas as pl
from jax.experimental.pallas import tpu as pltpu

RMS_EPS = 1e-6
LN_EPS = 1e-5
N_GROUPS = 4
EXPERTS_PER_GROUP = 8
N_EXPERTS = N_GROUPS * EXPERTS_PER_GROUP
ROUTER_ROWS = 40
LANES = 128
SUBLANES = 8
TOKEN_TILE = 512
CONF_HALO = 16
SC_HALO = 8
MOE_BLOCK = 512
CONV_ROWS = 128
CONF_PARTS = 1
TOKEN_DIGIT_BITS = 5
ROW_DMA_UNROLL = 8
SC_ISSUE_PIECES = 8
MOE_ISSUE_PIECES = 4
VMEM_LIMIT = 56 * 1024 * 1024

f32 = jnp.float32
bf16 = jnp.bfloat16


def _rms(x, g):
    ms = jnp.mean(x * x, axis=-1, keepdims=True)
    return x * lax.rsqrt(ms + RMS_EPS) * g


def _route(h, tile, wra_ref, wrb_ref, tri_ref, eidx_ref, gate_ref, lrank_ref, loc_ref):
    rows = h.shape[0]
    h_hi = h.astype(bf16)
    h_lo = (h - h_hi.astype(f32)).astype(bf16)
    dn = (((1,), (1,)), ((), ()))
    la = lax.dot_general(wra_ref[...], h_hi, dn, preferred_element_type=f32)
    lb = lax.dot_general(wrb_ref[...], h_lo, dn, preferred_element_type=f32)
    lt = la[:ROUTER_ROWS] + la[ROUTER_ROWS:] + lb
    el = lt[:N_EXPERTS]
    gl = lt[N_EXPERTS:N_EXPERTS + N_GROUPS]
    gm = jnp.max(gl, axis=0, keepdims=True)
    gp = jnp.exp(gl - gm)
    gprob = gp / jnp.sum(gp, axis=0, keepdims=True)
    g_top_p = jnp.max(gprob, axis=0, keepdims=True)
    iota_g = lax.broadcasted_iota(jnp.int32, (N_GROUPS, rows), 0)
    g_top = jnp.min(jnp.where(gprob == g_top_p, iota_g, N_GROUPS), axis=0, keepdims=True)
    elg = jnp.zeros((EXPERTS_PER_GROUP, rows), f32)
    for g in range(N_GROUPS):
        sl = el[g * EXPERTS_PER_GROUP:(g + 1) * EXPERTS_PER_GROUP]
        elg = elg + jnp.where(g_top == g, sl, 0.0)
    em = jnp.max(elg, axis=0, keepdims=True)
    ep = jnp.exp(elg - em)
    eprob = ep / jnp.sum(ep, axis=0, keepdims=True)
    iota_e = lax.broadcasted_iota(jnp.int32, (EXPERTS_PER_GROUP, rows), 0)
    p1 = jnp.max(eprob, axis=0, keepdims=True)
    i1 = jnp.min(jnp.where(eprob == p1, iota_e, EXPERTS_PER_GROUP), axis=0, keepdims=True)
    rest = jnp.where(iota_e == i1, -1.0, eprob)
    p2 = jnp.max(rest, axis=0, keepdims=True)
    i2 = jnp.min(jnp.where(rest == p2, iota_e, EXPERTS_PER_GROUP), axis=0, keepdims=True)
    den = p1 + p2
    e1 = g_top * EXPERTS_PER_GROUP + i1
    e2 = g_top * EXPERTS_PER_GROUP + i2
    eidx_ref[0:1, :] = e1
    eidx_ref[1:2, :] = e2
    gate_ref[0:1, :] = g_top_p * (p1 / den)
    gate_ref[1:2, :] = g_top_p * (p2 / den)

    iota_x = lax.broadcasted_iota(jnp.int32, (N_EXPERTS, rows), 0)
    oh = jnp.concatenate([iota_x == e1, iota_x == e2], axis=0).astype(f32)
    before = jnp.dot(oh.astype(bf16), tri_ref[...], preferred_element_type=f32)
    cnt1 = jnp.sum(oh[:N_EXPERTS], axis=1, keepdims=True)
    r1 = jnp.sum(oh[:N_EXPERTS] * before[:N_EXPERTS], axis=0, keepdims=True)
    r2 = jnp.sum(oh[N_EXPERTS:] * (before[N_EXPERTS:] + cnt1), axis=0, keepdims=True)
    lrank_ref[0:1, :] = r1.astype(jnp.int32)
    lrank_ref[1:2, :] = r2.astype(jnp.int32)

    cnt_all = cnt1 + jnp.sum(oh[N_EXPERTS:], axis=1, keepdims=True)
    s1 = (jnp.sum(jnp.where(iota_x < e1, cnt_all, 0.0), axis=0, keepdims=True) + r1).astype(jnp.int32)
    s2 = (jnp.sum(jnp.where(iota_x < e2, cnt_all, 0.0), axis=0, keepdims=True) + r2).astype(jnp.int32)
    iota_s = lax.broadcasted_iota(jnp.int32, (2 * rows, rows), 0)
    pmat = jnp.where((iota_s == s1) | (iota_s == s2), 1.0, 0.0).astype(bf16)
    tok = lax.broadcasted_iota(jnp.int32, (SUBLANES, rows), 1)
    digit = lax.broadcasted_iota(jnp.int32, (SUBLANES, rows), 0)
    digits = jnp.where(digit == 0, tok >> TOKEN_DIGIT_BITS,
                       jnp.where(digit == 1, tok & ((1 << TOKEN_DIGIT_BITS) - 1), 0))
    listed = lax.dot_general(digits.astype(f32).astype(bf16), pmat, dn, preferred_element_type=f32)
    loc = (1 << TOKEN_DIGIT_BITS) * listed[0:1] + listed[1:2]
    loc_ref[...] = loc.astype(jnp.int32) + tile * rows


def _seq_edges(i, n_tiles_a, tiles_per_seq_a, tiles_per_seq_b):
    in_a = i < n_tiles_a
    per = jnp.where(in_a, tiles_per_seq_a, tiles_per_seq_b)
    j = jnp.where(in_a, i, i - n_tiles_a) % per
    return j == 0, j == per - 1


def _const_spec(shape):
    nd = len(shape)
    return pl.BlockSpec(shape, lambda i: (0,) * nd, pipeline_mode=pl.Buffered(1))


def _row_tiles(x):
    return x.reshape(x.shape[0], x.shape[1] // LANES, LANES)


def _layer_out(t, d, ts):
    out_specs = [
        pl.BlockSpec((ts, d), lambda i: (i, 0)),
        pl.BlockSpec((ts, d // LANES, LANES), lambda i: (i, 0, 0)),
        pl.BlockSpec((2, ts), lambda i: (0, i)),
        pl.BlockSpec((2, ts), lambda i: (0, i)),
        pl.BlockSpec((2, ts), lambda i: (0, i)),
        pl.BlockSpec((1, 2 * ts), lambda i: (0, i)),
    ]
    out_shape = [
        jax.ShapeDtypeStruct((t, d), f32),
        jax.ShapeDtypeStruct((t, d // LANES, LANES), bf16),
        jax.ShapeDtypeStruct((2, t), jnp.int32),
        jax.ShapeDtypeStruct((2, t), f32),
        jax.ShapeDtypeStruct((2, t), jnp.int32),
        jax.ShapeDtypeStruct((1, 2 * t), jnp.int32),
    ]
    return out_specs, out_shape


def _conf_body(aprev_ref, amain_ref, anext_ref, bprev_ref, bmain_ref, bnext_ref,
               nmix_ref, w1_ref, b1_ref, wdw_ref, bdw_ref,
               lng_ref, lnb_ref, w2_ref, b2_ref, nffn_ref, wra_ref, wrb_ref, tri_ref,
               x1_ref, h_ref, eidx_ref, gate_ref, lrank_ref, loc_ref, xn_scr, v_scr, u_scr,
               *, n_tiles_a, tiles_per_seq_a, tiles_per_seq_b):
    ts, d = amain_ref.shape
    halo = CONF_HALO
    w = ts + 2 * halo
    n_col = d // LANES
    taps = wdw_ref.shape[0]
    i = pl.program_id(0)
    is_start, is_end = _seq_edges(i, n_tiles_a, tiles_per_seq_a, tiles_per_seq_b)
    in_a = i < n_tiles_a
    xmain = jnp.where(in_a, amain_ref[...], bmain_ref[...])

    g = nmix_ref[...]
    xn_scr[0:halo, :] = _rms(jnp.where(in_a, aprev_ref[...], bprev_ref[...]), g).astype(bf16)
    xn_scr[halo:halo + ts, :] = _rms(xmain, g).astype(bf16)
    xn_scr[halo + ts:w, :] = _rms(jnp.where(in_a, anext_ref[...], bnext_ref[...]), g).astype(bf16)
    n_part = v_scr.shape[0]
    part = ts // n_part
    pw = part + 2 * halo
    off = halo - taps // 2
    hs = []
    for s in range(n_part):
        r0 = s * part
        z = jnp.dot(xn_scr[r0:r0 + pw, :], w1_ref[...], preferred_element_type=f32) + b1_ref[...]
        v = z[:, :d] * jax.nn.sigmoid(z[:, d:])
        rows = r0 + lax.broadcasted_iota(jnp.int32, (pw, 1), 0)
        outside = (is_start & (rows < halo)) | (is_end & (rows >= halo + ts))
        v = jnp.where(outside, 0.0, v)
        for j in range(n_col):
            v_scr[s, j] = v[:, j * LANES:(j + 1) * LANES]

        for j in range(n_col):
            cs = slice(j * LANES, (j + 1) * LANES)
            wj = wdw_ref[:, cs]
            wb = [jnp.broadcast_to(wj[k:k + 1, :], (CONV_ROWS, LANES)) for k in range(taps)]
            bj = bdw_ref[:, cs]
            for c in range(part // CONV_ROWS):
                acc = v_scr[s, j, pl.ds(c * CONV_ROWS + off, CONV_ROWS), :] * wb[0]
                for k in range(1, taps):
                    acc = acc + v_scr[s, j, pl.ds(c * CONV_ROWS + off + k, CONV_ROWS), :] * wb[k]
                u_scr[pl.ds(r0 + c * CONV_ROWS, CONV_ROWS), cs] = acc + bj

        u = u_scr[r0:r0 + part, :]
        mu = jnp.mean(u, axis=-1, keepdims=True)
        uc = u - mu
        var = jnp.mean(uc * uc, axis=-1, keepdims=True)
        y = uc * lax.rsqrt(var + LN_EPS) * lng_ref[...] + lnb_ref[...]
        act = y * jax.nn.sigmoid(y)
        m = jnp.dot(act.astype(bf16), w2_ref[...], preferred_element_type=f32) + b2_ref[...]
        x1 = jnp.where(in_a, amain_ref[r0:r0 + part, :], bmain_ref[r0:r0 + part, :]) + m
        x1_ref[r0:r0 + part, :] = x1
        h = _rms(x1, nffn_ref[...])
        h_ref[r0:r0 + part] = _row_tiles(h.astype(bf16))
        hs.append(h)
    _route(jnp.concatenate(hs, axis=0), i, wra_ref, wrb_ref, tri_ref, eidx_ref, gate_ref, lrank_ref,
           loc_ref)


def _halo_specs(n_rows, d, ts, halo, tile_of_step):
    hb = ts // halo
    last = n_rows // halo - 1
    return [
        pl.BlockSpec((halo, d), lambda i: (jnp.maximum(tile_of_step(i) * hb - 1, 0), 0)),
        pl.BlockSpec((ts, d), lambda i: (tile_of_step(i), 0)),
        pl.BlockSpec((halo, d), lambda i: (jnp.minimum((tile_of_step(i) + 1) * hb, last), 0)),
    ]


def _conf_layer(xa, xb, seq_layout, nmix, w1, b1, wdw, bdw, lng, lnb, w2, b2, nffn, wra, wrb, tri):
    d = xa.shape[1]
    t = xa.shape[0] + xb.shape[0]
    ts = TOKEN_TILE
    n_tiles = t // ts
    n_tiles_a, tiles_per_seq_a, tiles_per_seq_b = seq_layout
    consts = (nmix, w1, b1, wdw, bdw, lng, lnb, w2, b2, nffn, wra, wrb, tri)
    in_specs = (
        _halo_specs(xa.shape[0], d, ts, CONF_HALO, lambda i: jnp.minimum(i, n_tiles_a - 1))
        + _halo_specs(xb.shape[0], d, ts, CONF_HALO, lambda i: jnp.maximum(i - n_tiles_a, 0))
        + [_const_spec(a.shape) for a in consts])
    out_specs, out_shape = _layer_out(t, d, ts)
    w = ts + 2 * CONF_HALO
    body = functools.partial(_conf_body, n_tiles_a=n_tiles_a, tiles_per_seq_a=tiles_per_seq_a,
                             tiles_per_seq_b=tiles_per_seq_b)
    return pl.pallas_call(
        body,
        grid=(n_tiles,),
        in_specs=in_specs,
        out_specs=out_specs,
        out_shape=out_shape,
        scratch_shapes=[
            pltpu.VMEM((w, d), bf16),
            pltpu.VMEM((CONF_PARTS, d // LANES, ts // CONF_PARTS + 2 * CONF_HALO, LANES), f32),
            pltpu.VMEM((ts, d), f32),
        ],
        compiler_params=pltpu.CompilerParams(
            dimension_semantics=("arbitrary",), vmem_limit_bytes=VMEM_LIMIT),
        name="conf_layer",
    )(xa, xa, xa, xb, xb, xb, *consts)


def _route_plan(eidx, lrank, loc, ts):
    t = eidx.shape[1]
    n_tiles = t // ts
    bm = MOE_BLOCK
    length = ((2 * t + N_EXPERTS * bm + bm - 1) // bm) * bm
    nb = length // bm
    experts = jnp.arange(N_EXPERTS, dtype=jnp.int32)
    tiles = jnp.arange(n_tiles, dtype=jnp.int32)
    eidx3 = eidx.reshape(2, n_tiles, ts)
    onehot = eidx3[..., None] == experts
    cnt = jnp.sum(onehot.astype(jnp.int32), axis=(0, 2))
    earlier = tiles[:, None] > tiles[None, :]
    tile_base = jnp.sum(jnp.where(earlier[:, :, None], cnt[None, :, :], 0), axis=1)
    counts = jnp.sum(cnt, axis=0)
    padded = ((counts + bm - 1) // bm) * bm
    pad_end = jnp.sum(jnp.where(experts[:, None] >= experts[None, :], padded[None, :], 0), axis=1)
    pad_start = pad_end - padded
    base = pad_start[None, :] + tile_base
    pos3 = jnp.sum(jnp.where(onehot, base[None, :, None, :], 0), axis=-1) + lrank.reshape(2, n_tiles, ts)
    pos = pos3.reshape(2, t).astype(jnp.int32)
    blk_start = jnp.arange(nb, dtype=jnp.int32) * bm
    block_e = jnp.minimum(jnp.sum((blk_start[:, None] >= pad_end[None, :]).astype(jnp.int32), axis=1),
                          N_EXPERTS - 1).astype(jnp.int32)
    n_used = (pad_end[-1] // bm).astype(jnp.int32).reshape(1)
    pos_tiles = pos.reshape(2, n_tiles, ts).transpose(1, 0, 2).reshape(n_tiles, 1, 2 * ts)

    lstart = jnp.sum(jnp.where(experts[:, None] > experts[None, :], cnt[:, None, :], 0), axis=2)
    tile_end = tile_base + cnt
    delta = tiles[:, None] * (2 * ts) + lstart - tile_base
    step = delta[1:] - delta[:-1]
    mine = block_e[:, None] == experts[None, :]
    pick = lambda table: jnp.sum(jnp.where(mine, table[None, :], 0), axis=1)
    pick_rows = lambda table: jnp.sum(jnp.where(mine[:, None, :], table[None, :, :], 0), axis=2)
    rank = (blk_start - pick(pad_start))[:, None] + jnp.arange(bm, dtype=jnp.int32)[None, :]
    passed = pick_rows(tile_end[:-1])[:, None, :] <= rank[:, :, None]
    src = rank + pick(delta[0])[:, None] + jnp.sum(jnp.where(passed, pick_rows(step)[:, None, :], 0), axis=2)
    valid = rank < pick(counts)[:, None]
    tok = jnp.where(valid, loc.reshape(-1)[jnp.where(valid, src, 0)], 0)
    return dict(pos=pos, pos_tiles=pos_tiles, block_e=block_e, n_used=n_used,
                tok=tok.reshape(nb, 1, bm).astype(jnp.int32), length=length)


def _window_tokens(t, ts, halo):
    n_tiles = t // ts
    w = ts + 2 * halo
    tok = jnp.arange(n_tiles, dtype=jnp.int32)[:, None] * ts - halo + jnp.arange(w, dtype=jnp.int32)[None, :]
    return jnp.clip(tok, 0, t - 1)


def _moe_body(be_ref, nused_ref, tok0_ref, tokn_ref, h_hbm, wg_ref, wu_ref, wd_ref, y_ref,
              buf, sem, wg_scr, wu_scr, wd_scr):
    b = pl.program_id(0)
    n_used = nused_ref[0]
    used = b < n_used
    _, bm, n_sub, n_lane = buf.shape

    def wait(s):
        pltpu.make_async_copy(buf.at[s], buf.at[s], sem.at[s]).wait()

    @pl.when(b == 0)
    def _():
        def body(r8, carry):
            for u in range(ROW_DMA_UNROLL):
                r = r8 * ROW_DMA_UNROLL + u
                pltpu.make_async_copy(h_hbm.at[tok0_ref[0, 0, r]], buf.at[0, r], sem.at[0]).start(priority=u % 2)
            return carry
        lax.fori_loop(0, bm // ROW_DMA_UNROLL, body, 0)

    @pl.when(used & ((b == 0) | (be_ref[b] != be_ref[jnp.maximum(b - 1, 0)])))
    def _():
        wg_scr[...] = wg_ref[0, 0].astype(bf16)
        wu_scr[...] = wu_ref[0, 0].astype(bf16)
        wd_scr[...] = wd_ref[0, 0].astype(bf16)

    @pl.when(used)
    def _():
        slot = b % 2
        nslot = 1 - slot
        piece = bm // MOE_ISSUE_PIECES

        def issue_next(q):
            for r in range(q * piece, (q + 1) * piece):
                pltpu.make_async_copy(h_hbm.at[tokn_ref[0, 0, r]], buf.at[nslot, r],
                                      sem.at[nslot]).start(priority=r % 2)

        wait(slot)
        x = buf[slot].reshape(bm, n_sub * n_lane)
        issue_next(0)
        issue_next(1)
        gt = jnp.dot(x, wg_scr[...], preferred_element_type=f32)
        issue_next(2)
        issue_next(3)
        up = jnp.dot(x, wu_scr[...], preferred_element_type=f32)
        hid = gt * jax.nn.sigmoid(gt) * up
        y = jnp.dot(hid.astype(bf16), wd_scr[...], preferred_element_type=f32)
        y_ref[...] = _row_tiles(y)

        @pl.when(b == n_used - 1)
        def _():
            wait(nslot)

    @pl.when(jnp.logical_not(used))
    def _():
        y_ref[...] = jnp.zeros_like(y_ref)


def _moe(h, plan, layer, w_gate, w_up, w_down):
    _, n_sub, n_lane = h.shape
    nb, _, bm = plan["tok"].shape
    _, _, d, d_exp = w_gate.shape
    w_in_spec = pl.BlockSpec((1, 1, d, d_exp), lambda b, be, nu: (layer, be[b], 0, 0))
    smem = pltpu.SMEM
    grid_spec = pltpu.PrefetchScalarGridSpec(
        num_scalar_prefetch=2,
        grid=(nb,),
        in_specs=[
            pl.BlockSpec((1, 1, bm), lambda b, be, nu: (0, 0, 0), memory_space=smem),
            pl.BlockSpec((1, 1, bm), lambda b, be, nu: (jnp.minimum(b + 1, nu[0] - 1), 0, 0),
                         memory_space=smem),
            pl.BlockSpec(memory_space=pl.ANY),
            w_in_spec,
            w_in_spec,
            pl.BlockSpec((1, 1, d_exp, d), lambda b, be, nu: (layer, be[b], 0, 0)),
        ],
        out_specs=pl.BlockSpec((bm, n_sub, n_lane), lambda b, be, nu: (b, 0, 0)),
        scratch_shapes=[pltpu.VMEM((2, bm, n_sub, n_lane), h.dtype), pltpu.SemaphoreType.DMA((2,)),
                        pltpu.VMEM((d, d_exp), bf16), pltpu.VMEM((d, d_exp), bf16),
                        pltpu.VMEM((d_exp, d), bf16)],
    )
    return pl.pallas_call(
        _moe_body,
        grid_spec=grid_spec,
        out_shape=jax.ShapeDtypeStruct((nb * bm, n_sub, n_lane), f32),
        compiler_params=pltpu.CompilerParams(
            dimension_semantics=("arbitrary",), vmem_limit_bytes=VMEM_LIMIT),
        name="moe",
    )(plan["block_e"], plan["n_used"], plan["tok"], plan["tok"], h, w_gate, w_up, w_down)


def _combine_gather(i, n_tiles, pos0_ref, posn_ref, y_hbm, gbuf, sem, spread=None):
    _, n_rows, n_sub, n_lane = gbuf.shape

    def issue(idx_ref, slot):
        def body(r8, carry):
            for u in range(ROW_DMA_UNROLL):
                r = r8 * ROW_DMA_UNROLL + u
                p = idx_ref[0, 0, r]
                pltpu.make_async_copy(y_hbm.at[p], gbuf.at[slot, r], sem.at[slot]).start(priority=u % 2)
            return carry
        lax.fori_loop(0, n_rows // ROW_DMA_UNROLL, body, 0)

    @pl.when(i == 0)
    def _():
        issue(pos0_ref, 0)

    slot = i % 2
    nslot = 1 - slot

    def wait(s):
        pltpu.make_async_copy(gbuf.at[s], gbuf.at[s], sem.at[s]).wait()

    if spread is None:
        @pl.when(i + 1 < n_tiles)
        def _():
            issue(posn_ref, nslot)
        issue_next = None
    else:
        piece = -(-n_rows // spread)

        def issue_next(q):
            for r in range(q * piece, min((q + 1) * piece, n_rows)):
                p = posn_ref[0, 0, r]
                pltpu.make_async_copy(y_hbm.at[p], gbuf.at[nslot, r], sem.at[nslot]).start(priority=r % 2)
            if q == spread - 1:
                @pl.when(i == n_tiles - 1)
                def _():
                    wait(nslot)

    wait(slot)
    return gbuf[slot].reshape(n_rows, n_sub * n_lane), issue_next


def _sc_body(pos0_ref, posn_ref, xprev_ref, xmain_ref, xnext_ref, gw_ref, y_hbm, nmix_ref, win_ref,
             wdw_ref, wout_ref, nffn_ref, wra_ref, wrb_ref, tri_ref,
             x2_ref, h_ref, eidx_ref, gate_ref, lrank_ref, loc_ref, gbuf, sem, xin_scr, cu_scr,
             *, n_tiles, n_tiles_a, tiles_per_seq_a, tiles_per_seq_b):
    ts, d = xmain_ref.shape
    halo = SC_HALO
    w = ts + 2 * halo
    n_col = d // LANES
    i = pl.program_id(0)
    is_start, is_end = _seq_edges(i, n_tiles_a, tiles_per_seq_a, tiles_per_seq_b)
    yrows, issue_next = _combine_gather(i, n_tiles, pos0_ref, posn_ref, y_hbm, gbuf, sem,
                                        spread=SC_ISSUE_PIECES)

    xin_scr[0:halo, :] = xprev_ref[...]
    xin_scr[halo:halo + ts, :] = xmain_ref[...]
    xin_scr[halo + ts:w, :] = xnext_ref[...]
    gw = gw_ref[0]
    xc = xin_scr[...] + gw[:, 0:1] * yrows[0:w] + gw[:, 1:2] * yrows[w:2 * w]
    issue_next(0)
    issue_next(1)
    hn = _rms(xc, nmix_ref[...]).astype(bf16)
    issue_next(2)
    z = jnp.dot(hn, win_ref[...], preferred_element_type=f32)
    issue_next(3)
    issue_next(4)
    cu = z[:, d:2 * d] * z[:, 2 * d:]
    rows = lax.broadcasted_iota(jnp.int32, (w, 1), 0)
    outside = (is_start & (rows < halo)) | (is_end & (rows >= halo + ts))
    cu = jnp.where(outside, 0.0, cu)
    for j in range(n_col):
        cu_scr[j] = cu[:, j * LANES:(j + 1) * LANES]
    issue_next(5)
    taps = wdw_ref.shape[0]
    off = halo - taps // 2
    parts = []
    for j in range(n_col):
        cs = slice(j * LANES, (j + 1) * LANES)
        wj = wdw_ref[:, cs]
        acc = cu_scr[j, pl.ds(off, ts), :] * wj[0:1, :]
        for k in range(1, taps):
            acc = acc + cu_scr[j, pl.ds(off + k, ts), :] * wj[k:k + 1, :]
        parts.append(acc)
    conv = jnp.concatenate(parts, axis=1)
    bv = z[halo:halo + ts, :d] * conv
    issue_next(6)
    m = jnp.dot(bv.astype(bf16), wout_ref[...], preferred_element_type=f32)
    issue_next(7)
    x2 = xc[halo:halo + ts, :] + m
    x2_ref[...] = x2
    h = _rms(x2, nffn_ref[...])
    h_ref[...] = _row_tiles(h.astype(bf16))
    _route(h, i, wra_ref, wrb_ref, tri_ref, eidx_ref, gate_ref, lrank_ref, loc_ref)


def _sc_layer(x1, y, posw, gw, seq_layout, nmix, win, wdw, wout, nffn, wra, wrb, tri):
    t, d = x1.shape
    ts = TOKEN_TILE
    n_tiles = t // ts
    n_tiles_a, tiles_per_seq_a, tiles_per_seq_b = seq_layout
    hb = ts // SC_HALO
    n_hblk = t // SC_HALO
    w = ts + 2 * SC_HALO
    smem = pltpu.SMEM
    consts = (nmix, win, wdw, wout, nffn, wra, wrb, tri)
    in_specs = [
        pl.BlockSpec((1, 1, 2 * w), lambda i: (0, 0, 0), memory_space=smem),
        pl.BlockSpec((1, 1, 2 * w), lambda i: (jnp.minimum(i + 1, n_tiles - 1), 0, 0), memory_space=smem),
        pl.BlockSpec((SC_HALO, d), lambda i: (jnp.maximum(i * hb - 1, 0), 0)),
        pl.BlockSpec((ts, d), lambda i: (i, 0)),
        pl.BlockSpec((SC_HALO, d), lambda i: (jnp.minimum((i + 1) * hb, n_hblk - 1), 0)),
        pl.BlockSpec((1, w, 2), lambda i: (i, 0, 0)),
        pl.BlockSpec(memory_space=pl.ANY),
    ] + [_const_spec(a.shape) for a in consts]
    out_specs, out_shape = _layer_out(t, d, ts)
    body = functools.partial(_sc_body, n_tiles=n_tiles, n_tiles_a=n_tiles_a,
                             tiles_per_seq_a=tiles_per_seq_a, tiles_per_seq_b=tiles_per_seq_b)
    return pl.pallas_call(
        body,
        grid=(n_tiles,),
        in_specs=in_specs,
        out_specs=out_specs,
        out_shape=out_shape,
        scratch_shapes=[
            pltpu.VMEM((2, 2 * w, d // LANES, LANES), f32),
            pltpu.SemaphoreType.DMA((2,)),
            pltpu.VMEM((w, d), f32),
            pltpu.VMEM((d // LANES, w, LANES), f32),
        ],
        compiler_params=pltpu.CompilerParams(
            dimension_semantics=("arbitrary",), vmem_limit_bytes=VMEM_LIMIT),
        name="sc_layer",
    )(posw, posw, x1, x1, x1, gw, y, *consts)


def _final_body(pos0_ref, posn_ref, x_ref, gt_ref, y_hbm, g_ref, o_ref, gbuf, sem, *, n_tiles):
    ts = x_ref.shape[0]
    i = pl.program_id(0)
    yrows, _ = _combine_gather(i, n_tiles, pos0_ref, posn_ref, y_hbm, gbuf, sem)
    gt = gt_ref[...]
    xc = x_ref[...] + gt[:, 0:1] * yrows[0:ts] + gt[:, 1:2] * yrows[ts:2 * ts]
    o_ref[...] = _rms(xc, g_ref[...])


def _final(x2, y, pos_tiles, gates_t, g, tile_off, n_tiles):
    t, d = x2.shape
    ts = TOKEN_TILE
    smem = pltpu.SMEM
    in_specs = [
        pl.BlockSpec((1, 1, 2 * ts), lambda i: (tile_off, 0, 0), memory_space=smem),
        pl.BlockSpec((1, 1, 2 * ts), lambda i: (tile_off + jnp.minimum(i + 1, n_tiles - 1), 0, 0),
                     memory_space=smem),
        pl.BlockSpec((ts, d), lambda i: (tile_off + i, 0)),
        pl.BlockSpec((ts, 2), lambda i: (tile_off + i, 0)),
        pl.BlockSpec(memory_space=pl.ANY),
        _const_spec(g.shape),
    ]
    return pl.pallas_call(
        functools.partial(_final_body, n_tiles=n_tiles),
        grid=(n_tiles,),
        in_specs=in_specs,
        out_specs=pl.BlockSpec((ts, d), lambda i: (i, 0)),
        out_shape=jax.ShapeDtypeStruct((n_tiles * ts, d), f32),
        scratch_shapes=[pltpu.VMEM((2, 2 * ts, d // LANES, LANES), f32),
                        pltpu.SemaphoreType.DMA((2,))],
        compiler_params=pltpu.CompilerParams(
            dimension_semantics=("arbitrary",), vmem_limit_bytes=VMEM_LIMIT),
        name="final_norm",
    )(pos_tiles, pos_tiles, x2, gates_t, y, g)


def _router_weights(w_rg, w_re):
    d = w_rg.shape[0]
    wt = jnp.concatenate([w_re.T, w_rg.T, jnp.zeros((ROUTER_ROWS - N_EXPERTS - N_GROUPS, d), f32)], axis=0)
    hi = wt.astype(bf16)
    lo = (wt - hi.astype(f32)).astype(bf16)
    return jnp.concatenate([hi, lo], axis=0), hi


def kernel(x_prompt, x_sample, norm_mix, conf_w_pw1, conf_b_pw1, conf_w_dw, conf_b_dw, conf_ln_g,
           conf_ln_b, conf_w_pw2, conf_b_pw2, sc_w_in, sc_w_dw, sc_w_out, norm_ffn, router_group,
           router_expert, w_gate, w_up, w_down, final_norm):
    bp, sp, d = x_prompt.shape
    bs, ss, _ = x_sample.shape
    ts = TOKEN_TILE
    assert norm_mix.shape[0] == 2 and sp % ts == 0 and ss % ts == 0 and d % LANES == 0
    tp, tsamp = bp * sp, bs * ss
    t = tp + tsamp
    seq_layout = (tp // ts, sp // ts, ss // ts)

    row = lambda a: a.reshape(1, -1)
    wra0, wrb0 = _router_weights(router_group[0], router_expert[0])
    wra1, wrb1 = _router_weights(router_group[1], router_expert[1])
    ti = jnp.arange(ts, dtype=jnp.int32)
    tri = (ti[:, None] < ti[None, :]).astype(bf16)

    x1, h1, eidx1, gates1, lrank1, loc1 = _conf_layer(
        x_prompt.reshape(tp, d), x_sample.reshape(tsamp, d), seq_layout, row(norm_mix[0]), conf_w_pw1[0].astype(bf16), row(conf_b_pw1[0]),
        conf_w_dw[0], row(conf_b_dw[0]), row(conf_ln_g[0]), row(conf_ln_b[0]),
        conf_w_pw2[0].astype(bf16), row(conf_b_pw2[0]), row(norm_ffn[0]), wra0, wrb0, tri)
    plan1 = _route_plan(eidx1, lrank1, loc1, ts)
    y1 = _moe(h1, plan1, 0, w_gate, w_up, w_down)

    tokw = _window_tokens(t, ts, SC_HALO)
    posw1 = jnp.concatenate([plan1["pos"][0][tokw], plan1["pos"][1][tokw]], axis=1)
    posw1 = posw1.reshape(t // ts, 1, 2 * tokw.shape[1])
    gw1 = gates1.T[tokw]
    x2, h2, eidx2, gates2, lrank2, loc2 = _sc_layer(
        x1, y1, posw1, gw1, seq_layout, row(norm_mix[1]), sc_w_in[0].astype(bf16), sc_w_dw[0],
        sc_w_out[0].astype(bf16), row(norm_ffn[1]), wra1, wrb1, tri)
    plan2 = _route_plan(eidx2, lrank2, loc2, ts)
    y2 = _moe(h2, plan2, 1, w_gate, w_up, w_down)

    fg = row(final_norm)
    gates2_t = gates2.T
    out_p = _final(x2, y2, plan2["pos_tiles"], gates2_t, fg, 0, tp // ts)
    out_s = _final(x2, y2, plan2["pos_tiles"], gates2_t, fg, tp // ts, tsamp // ts)
    return out_p.reshape(bp, sp, d), out_s.reshape(bs, ss, d)
```

```python
import functools

import jax
import jax.numpy as jnp
from jax import lax
from jax.experimental import pallas as pl
from jax.experimental.pallas import tpu as pltpu

RMS_EPS = 1e-6
LN_EPS = 1e-5
N_GROUPS = 4
EXPERTS_PER_GROUP = 8
N_EXPERTS = N_GROUPS * EXPERTS_PER_GROUP
ROUTER_ROWS = 40
LANES = 128
SUBLANES = 8
TOKEN_TILE = 512
CONF_HALO = 16
SC_HALO = 8
MOE_BLOCK = 512
CONV_ROWS = 128
ROW_DMA_UNROLL = 8
SC_ISSUE_PIECES = 8
VMEM_LIMIT = 56 * 1024 * 1024

f32 = jnp.float32
bf16 = jnp.bfloat16


def _rms(x, g):
    ms = jnp.mean(x * x, axis=-1, keepdims=True)
    return x * lax.rsqrt(ms + RMS_EPS) * g


def _route(h, wra_ref, wrb_ref, tri_ref, eidx_ref, gate_ref, lrank_ref):
    rows = h.shape[0]
    h_hi = h.astype(bf16)
    h_lo = (h - h_hi.astype(f32)).astype(bf16)
    dn = (((1,), (1,)), ((), ()))
    la = lax.dot_general(wra_ref[...], h_hi, dn, preferred_element_type=f32)
    lb = lax.dot_general(wrb_ref[...], h_lo, dn, preferred_element_type=f32)
    lt = la[:ROUTER_ROWS] + la[ROUTER_ROWS:] + lb
    el = lt[:N_EXPERTS]
    gl = lt[N_EXPERTS:N_EXPERTS + N_GROUPS]
    gm = jnp.max(gl, axis=0, keepdims=True)
    gp = jnp.exp(gl - gm)
    gprob = gp / jnp.sum(gp, axis=0, keepdims=True)
    g_top_p = jnp.max(gprob, axis=0, keepdims=True)
    iota_g = lax.broadcasted_iota(jnp.int32, (N_GROUPS, rows), 0)
    g_top = jnp.min(jnp.where(gprob == g_top_p, iota_g, N_GROUPS), axis=0, keepdims=True)
    elg = jnp.zeros((EXPERTS_PER_GROUP, rows), f32)
    for g in range(N_GROUPS):
        sl = el[g * EXPERTS_PER_GROUP:(g + 1) * EXPERTS_PER_GROUP]
        elg = elg + jnp.where(g_top == g, sl, 0.0)
    em = jnp.max(elg, axis=0, keepdims=True)
    ep = jnp.exp(elg - em)
    eprob = ep / jnp.sum(ep, axis=0, keepdims=True)
    iota_e = lax.broadcasted_iota(jnp.int32, (EXPERTS_PER_GROUP, rows), 0)
    p1 = jnp.max(eprob, axis=0, keepdims=True)
    i1 = jnp.min(jnp.where(eprob == p1, iota_e, EXPERTS_PER_GROUP), axis=0, keepdims=True)
    rest = jnp.where(iota_e == i1, -1.0, eprob)
    p2 = jnp.max(rest, axis=0, keepdims=True)
    i2 = jnp.min(jnp.where(rest == p2, iota_e, EXPERTS_PER_GROUP), axis=0, keepdims=True)
    den = p1 + p2
    e1 = g_top * EXPERTS_PER_GROUP + i1
    e2 = g_top * EXPERTS_PER_GROUP + i2
    eidx_ref[0:1, :] = e1
    eidx_ref[1:2, :] = e2
    gate_ref[0:1, :] = g_top_p * (p1 / den)
    gate_ref[1:2, :] = g_top_p * (p2 / den)

    iota_x = lax.broadcasted_iota(jnp.int32, (N_EXPERTS, rows), 0)
    oh = jnp.concatenate([iota_x == e1, iota_x == e2], axis=0).astype(f32)
    before = jnp.dot(oh.astype(bf16), tri_ref[...], preferred_element_type=f32)
    cnt1 = jnp.sum(oh[:N_EXPERTS], axis=1, keepdims=True)
    r1 = jnp.sum(oh[:N_EXPERTS] * before[:N_EXPERTS], axis=0, keepdims=True)
    r2 = jnp.sum(oh[N_EXPERTS:] * (before[N_EXPERTS:] + cnt1), axis=0, keepdims=True)
    lrank_ref[0:1, :] = r1.astype(jnp.int32)
    lrank_ref[1:2, :] = r2.astype(jnp.int32)


def _seq_edges(i, n_tiles_a, tiles_per_seq_a, tiles_per_seq_b):
    in_a = i < n_tiles_a
    per = jnp.where(in_a, tiles_per_seq_a, tiles_per_seq_b)
    j = jnp.where(in_a, i, i - n_tiles_a) % per
    return j == 0, j == per - 1


def _const_spec(shape):
    nd = len(shape)
    return pl.BlockSpec(shape, lambda i: (0,) * nd, pipeline_mode=pl.Buffered(1))


def _row_tiles(x):
    return x.reshape(x.shape[0], x.shape[1] // LANES, LANES)


def _layer_out(t, d, ts):
    out_specs = [
        pl.BlockSpec((ts, d), lambda i: (i, 0)),
        pl.BlockSpec((ts, d // LANES, LANES), lambda i: (i, 0, 0)),
        pl.BlockSpec((2, ts), lambda i: (0, i)),
        pl.BlockSpec((2, ts), lambda i: (0, i)),
        pl.BlockSpec((2, ts), lambda i: (0, i)),
    ]
    out_shape = [
        jax.ShapeDtypeStruct((t, d), f32),
        jax.ShapeDtypeStruct((t, d // LANES, LANES), bf16),
        jax.ShapeDtypeStruct((2, t), jnp.int32),
        jax.ShapeDtypeStruct((2, t), f32),
        jax.ShapeDtypeStruct((2, t), jnp.int32),
    ]
    return out_specs, out_shape


def _conf_body(aprev_ref, amain_ref, anext_ref, bprev_ref, bmain_ref, bnext_ref,
               nmix_ref, w1_ref, b1_ref, wdw_ref, bdw_ref,
               lng_ref, lnb_ref, w2_ref, b2_ref, nffn_ref, wra_ref, wrb_ref, tri_ref,
               x1_ref, h_ref, eidx_ref, gate_ref, lrank_ref, xn_scr, v_scr, u_scr,
               *, n_tiles_a, tiles_per_seq_a, tiles_per_seq_b):
    ts, d = amain_ref.shape
    halo = CONF_HALO
    w = ts + 2 * halo
    n_col = d // LANES
    taps = wdw_ref.shape[0]
    i = pl.program_id(0)
    is_start, is_end = _seq_edges(i, n_tiles_a, tiles_per_seq_a, tiles_per_seq_b)
    in_a = i < n_tiles_a
    xmain = jnp.where(in_a, amain_ref[...], bmain_ref[...])

    g = nmix_ref[...]
    xn_scr[0:halo, :] = _rms(jnp.where(in_a, aprev_ref[...], bprev_ref[...]), g).astype(bf16)
    xn_scr[halo:halo + ts, :] = _rms(xmain, g).astype(bf16)
    xn_scr[halo + ts:w, :] = _rms(jnp.where(in_a, anext_ref[...], bnext_ref[...]), g).astype(bf16)
    z = jnp.dot(xn_scr[...], w1_ref[...], preferred_element_type=f32) + b1_ref[...]
    v = z[:, :d] * jax.nn.sigmoid(z[:, d:])
    rows = lax.broadcasted_iota(jnp.int32, (w, 1), 0)
    outside = (is_start & (rows < halo)) | (is_end & (rows >= halo + ts))
    v = jnp.where(outside, 0.0, v)
    for j in range(n_col):
        v_scr[j] = v[:, j * LANES:(j + 1) * LANES]

    off = halo - taps // 2
    for j in range(n_col):
        cs = slice(j * LANES, (j + 1) * LANES)
        wj = wdw_ref[:, cs]
        wb = [jnp.broadcast_to(wj[k:k + 1, :], (CONV_ROWS, LANES)) for k in range(taps)]
        bj = bdw_ref[:, cs]
        for c in range(ts // CONV_ROWS):
            acc = v_scr[j, pl.ds(c * CONV_ROWS + off, CONV_ROWS), :] * wb[0]
            for k in range(1, taps):
                acc = acc + v_scr[j, pl.ds(c * CONV_ROWS + off + k, CONV_ROWS), :] * wb[k]
            u_scr[pl.ds(c * CONV_ROWS, CONV_ROWS), cs] = acc + bj

    u = u_scr[...]
    mu = jnp.mean(u, axis=-1, keepdims=True)
    uc = u - mu
    var = jnp.mean(uc * uc, axis=-1, keepdims=True)
    y = uc * lax.rsqrt(var + LN_EPS) * lng_ref[...] + lnb_ref[...]
    s = y * jax.nn.sigmoid(y)
    m = jnp.dot(s.astype(bf16), w2_ref[...], preferred_element_type=f32) + b2_ref[...]
    x1 = jnp.where(in_a, amain_ref[...], bmain_ref[...]) + m
    x1_ref[...] = x1
    h = _rms(x1, nffn_ref[...])
    h_ref[...] = _row_tiles(h.astype(bf16))
    _route(h, wra_ref, wrb_ref, tri_ref, eidx_ref, gate_ref, lrank_ref)


def _halo_specs(n_rows, d, ts, halo, tile_of_step):
    hb = ts // halo
    last = n_rows // halo - 1
    return [
        pl.BlockSpec((halo, d), lambda i: (jnp.maximum(tile_of_step(i) * hb - 1, 0), 0)),
        pl.BlockSpec((ts, d), lambda i: (tile_of_step(i), 0)),
        pl.BlockSpec((halo, d), lambda i: (jnp.minimum((tile_of_step(i) + 1) * hb, last), 0)),
    ]


def _conf_layer(xa, xb, seq_layout, nmix, w1, b1, wdw, bdw, lng, lnb, w2, b2, nffn, wra, wrb, tri):
    d = xa.shape[1]
    t = xa.shape[0] + xb.shape[0]
    ts = TOKEN_TILE
    n_tiles = t // ts
    n_tiles_a, tiles_per_seq_a, tiles_per_seq_b = seq_layout
    consts = (nmix, w1, b1, wdw, bdw, lng, lnb, w2, b2, nffn, wra, wrb, tri)
    in_specs = (
        _halo_specs(xa.shape[0], d, ts, CONF_HALO, lambda i: jnp.minimum(i, n_tiles_a - 1))
        + _halo_specs(xb.shape[0], d, ts, CONF_HALO, lambda i: jnp.maximum(i - n_tiles_a, 0))
        + [_const_spec(a.shape) for a in consts])
    out_specs, out_shape = _layer_out(t, d, ts)
    w = ts + 2 * CONF_HALO
    body = functools.partial(_conf_body, n_tiles_a=n_tiles_a, tiles_per_seq_a=tiles_per_seq_a,
                             tiles_per_seq_b=tiles_per_seq_b)
    return pl.pallas_call(
        body,
        grid=(n_tiles,),
        in_specs=in_specs,
        out_specs=out_specs,
        out_shape=out_shape,
        scratch_shapes=[
            pltpu.VMEM((w, d), bf16),
            pltpu.VMEM((d // LANES, w, LANES), f32),
            pltpu.VMEM((ts, d), f32),
        ],
        compiler_params=pltpu.CompilerParams(
            dimension_semantics=("arbitrary",), vmem_limit_bytes=VMEM_LIMIT),
        name="conf_layer",
    )(xa, xa, xa, xb, xb, xb, *consts)


def _route_plan(eidx, lrank, ts):
    t = eidx.shape[1]
    n_tiles = t // ts
    bm = MOE_BLOCK
    length = ((2 * t + N_EXPERTS * bm + bm - 1) // bm) * bm
    nb = length // bm
    experts = jnp.arange(N_EXPERTS, dtype=jnp.int32)
    tiles = jnp.arange(n_tiles, dtype=jnp.int32)
    eidx3 = eidx.reshape(2, n_tiles, ts)
    onehot = eidx3[..., None] == experts
    cnt = jnp.sum(onehot.astype(jnp.int32), axis=(0, 2))
    earlier = tiles[:, None] > tiles[None, :]
    tile_base = jnp.sum(jnp.where(earlier[:, :, None], cnt[None, :, :], 0), axis=1)
    counts = jnp.sum(cnt, axis=0)
    padded = ((counts + bm - 1) // bm) * bm
    pad_end = jnp.sum(jnp.where(experts[:, None] >= experts[None, :], padded[None, :], 0), axis=1)
    pad_start = pad_end - padded
    base = pad_start[None, :] + tile_base
    pos3 = jnp.sum(jnp.where(onehot, base[None, :, None, :], 0), axis=-1) + lrank.reshape(2, n_tiles, ts)
    pos = pos3.reshape(2, t).astype(jnp.int32)
    blk_start = jnp.arange(nb, dtype=jnp.int32) * bm
    block_e = jnp.minimum(jnp.sum((blk_start[:, None] >= pad_end[None, :]).astype(jnp.int32), axis=1),
                          N_EXPERTS - 1).astype(jnp.int32)
    n_used = (pad_end[-1] // bm).astype(jnp.int32).reshape(1)
    pos_tiles = pos.reshape(2, n_tiles, ts).transpose(1, 0, 2).reshape(n_tiles, 1, 2 * ts)
    tail = n_used[0] + experts
    zero_off = jnp.concatenate([pad_end - bm, tail * bm])
    zero_on = jnp.concatenate([padded > 0, tail < nb])
    zero_off = jnp.where(zero_on, zero_off, -1).astype(jnp.int32)
    return dict(pos=pos, pos_tiles=pos_tiles, block_e=block_e, n_used=n_used,
                zero_off=zero_off, length=length)


def _window_tokens(t, ts, halo):
    n_tiles = t // ts
    w = ts + 2 * halo
    tok = jnp.arange(n_tiles, dtype=jnp.int32)[:, None] * ts - halo + jnp.arange(w, dtype=jnp.int32)[None, :]
    return jnp.clip(tok, 0, t - 1)


def _dispatch_body(zoff_ref, pos_ref, h_ref, hs_hbm, zbuf, sem, zsem):
    ts = h_ref.shape[0]
    bm = zbuf.shape[0]
    n_zero = zoff_ref.shape[0]

    @pl.when(pl.program_id(0) == 0)
    def _():
        zbuf[...] = jnp.zeros_like(zbuf)

        def zero_copy(j):
            off = pl.multiple_of(zoff_ref[j], bm)
            return pltpu.make_async_copy(zbuf, hs_hbm.at[pl.ds(off, bm)], zsem)

        def start(j, carry):
            @pl.when(zoff_ref[j] >= 0)
            def _():
                zero_copy(j).start()
            return carry

        def wait(j, carry):
            @pl.when(zoff_ref[j] >= 0)
            def _():
                zero_copy(j).wait()
            return carry

        lax.fori_loop(0, n_zero, start, 0)
        lax.fori_loop(0, n_zero, wait, 0)

    def body(r8, carry):
        for u in range(ROW_DMA_UNROLL):
            r = r8 * ROW_DMA_UNROLL + u
            for k in range(2):
                p = pos_ref[0, 0, k * ts + r]
                pltpu.make_async_copy(h_ref.at[r], hs_hbm.at[p], sem).start(priority=(2 * u + k) % 2)
        return carry

    lax.fori_loop(0, ts // ROW_DMA_UNROLL, body, 0)
    for _ in range(2):
        pltpu.make_async_copy(hs_hbm.at[pl.ds(0, ts)], hs_hbm.at[pl.ds(ts, ts)], sem).wait()


def _dispatch(h, plan):
    t, n_sub, n_lane = h.shape
    pos_tiles = plan["pos_tiles"]
    n_tiles = pos_tiles.shape[0]
    ts = t // n_tiles
    grid_spec = pltpu.PrefetchScalarGridSpec(
        num_scalar_prefetch=1,
        grid=(n_tiles,),
        in_specs=[
            pl.BlockSpec((1, 1, 2 * ts), lambda i, z: (i, 0, 0), memory_space=pltpu.SMEM),
            pl.BlockSpec((ts, n_sub, n_lane), lambda i, z: (i, 0, 0)),
        ],
        out_specs=pl.BlockSpec(memory_space=pl.ANY),
        scratch_shapes=[pltpu.VMEM((MOE_BLOCK, n_sub, n_lane), h.dtype), pltpu.SemaphoreType.DMA(()),
                        pltpu.SemaphoreType.DMA(())],
    )
    return pl.pallas_call(
        _dispatch_body,
        grid_spec=grid_spec,
        out_shape=jax.ShapeDtypeStruct((plan["length"], n_sub, n_lane), h.dtype),
        compiler_params=pltpu.CompilerParams(dimension_semantics=("arbitrary",)),
        name="dispatch",
    )(plan["zero_off"], pos_tiles, h)


def _moe_body(be_ref, nused_ref, hs_ref, wg_ref, wu_ref, wd_ref, y_ref, wg_scr, wu_scr, wd_scr):
    b = pl.program_id(0)
    used = b < nused_ref[0]

    @pl.when(used & ((b == 0) | (be_ref[b] != be_ref[jnp.maximum(b - 1, 0)])))
    def _():
        wg_scr[...] = wg_ref[0, 0].astype(bf16)
        wu_scr[...] = wu_ref[0, 0].astype(bf16)
        wd_scr[...] = wd_ref[0, 0].astype(bf16)

    @pl.when(used)
    def _():
        bm, n_sub, n_lane = hs_ref.shape
        x = hs_ref[...].reshape(bm, n_sub * n_lane)
        gt = jnp.dot(x, wg_scr[...], preferred_element_type=f32)
        up = jnp.dot(x, wu_scr[...], preferred_element_type=f32)
        hid = gt * jax.nn.sigmoid(gt) * up
        y = jnp.dot(hid.astype(bf16), wd_scr[...], preferred_element_type=f32)
        y_ref[...] = _row_tiles(y.astype(y_ref.dtype))

    @pl.when(jnp.logical_not(used))
    def _():
        y_ref[...] = jnp.zeros_like(y_ref)


def _moe(hs, plan, layer, w_gate, w_up, w_down):
    length, n_sub, n_lane = hs.shape
    bm = MOE_BLOCK
    nb = length // bm
    _, _, d, d_exp = w_gate.shape
    w_in_spec = pl.BlockSpec((1, 1, d, d_exp), lambda b, be, nu: (layer, be[b], 0, 0))
    grid_spec = pltpu.PrefetchScalarGridSpec(
        num_scalar_prefetch=2,
        grid=(nb,),
        in_specs=[
            pl.BlockSpec((bm, n_sub, n_lane), lambda b, be, nu: (jnp.minimum(b, nu[0] - 1), 0, 0)),
            w_in_spec,
            w_in_spec,
            pl.BlockSpec((1, 1, d_exp, d), lambda b, be, nu: (layer, be[b], 0, 0)),
        ],
        out_specs=pl.BlockSpec((bm, n_sub, n_lane), lambda b, be, nu: (b, 0, 0)),
        scratch_shapes=[pltpu.VMEM((d, d_exp), bf16), pltpu.VMEM((d, d_exp), bf16),
                        pltpu.VMEM((d_exp, d), bf16)],
    )
    return pl.pallas_call(
        _moe_body,
        grid_spec=grid_spec,
        out_shape=jax.ShapeDtypeStruct((length, n_sub, n_lane), bf16),
        compiler_params=pltpu.CompilerParams(
            dimension_semantics=("arbitrary",), vmem_limit_bytes=VMEM_LIMIT),
        name="moe",
    )(plan["block_e"], plan["n_used"], hs, w_gate, w_up, w_down)


def _combine_gather(i, n_tiles, pos0_ref, posn_ref, y_hbm, gbuf, sem, spread=None):
    _, n_rows, n_sub, n_lane = gbuf.shape

    def issue(idx_ref, slot):
        def body(r8, carry):
            for u in range(ROW_DMA_UNROLL):
                r = r8 * ROW_DMA_UNROLL + u
                p = idx_ref[0, 0, r]
                pltpu.make_async_copy(y_hbm.at[p], gbuf.at[slot, r], sem.at[slot]).start(priority=u % 2)
            return carry
        lax.fori_loop(0, n_rows // ROW_DMA_UNROLL, body, 0)

    @pl.when(i == 0)
    def _():
        issue(pos0_ref, 0)

    slot = i % 2
    nslot = 1 - slot

    def wait(s):
        pltpu.make_async_copy(gbuf.at[s], gbuf.at[s], sem.at[s]).wait()

    if spread is None:
        @pl.when(i + 1 < n_tiles)
        def _():
            issue(posn_ref, nslot)
        issue_next = None
    else:
        piece = -(-n_rows // spread)

        def issue_next(q):
            for r in range(q * piece, min((q + 1) * piece, n_rows)):
                p = posn_ref[0, 0, r]
                pltpu.make_async_copy(y_hbm.at[p], gbuf.at[nslot, r], sem.at[nslot]).start(priority=r % 2)
            if q == spread - 1:
                @pl.when(i == n_tiles - 1)
                def _():
                    wait(nslot)

    wait(slot)
    return gbuf[slot].reshape(n_rows, n_sub * n_lane).astype(f32), issue_next


def _sc_body(pos0_ref, posn_ref, xprev_ref, xmain_ref, xnext_ref, gw_ref, y_hbm, nmix_ref, win_ref,
             wdw_ref, wout_ref, nffn_ref, wra_ref, wrb_ref, tri_ref,
             x2_ref, h_ref, eidx_ref, gate_ref, lrank_ref, gbuf, sem, xin_scr, cu_scr,
             *, n_tiles, n_tiles_a, tiles_per_seq_a, tiles_per_seq_b):
    ts, d = xmain_ref.shape
    halo = SC_HALO
    w = ts + 2 * halo
    n_col = d // LANES
    i = pl.program_id(0)
    is_start, is_end = _seq_edges(i, n_tiles_a, tiles_per_seq_a, tiles_per_seq_b)
    yrows, issue_next = _combine_gather(i, n_tiles, pos0_ref, posn_ref, y_hbm, gbuf, sem,
                                        spread=SC_ISSUE_PIECES)

    xin_scr[0:halo, :] = xprev_ref[...]
    xin_scr[halo:halo + ts, :] = xmain_ref[...]
    xin_scr[halo + ts:w, :] = xnext_ref[...]
    gw = gw_ref[0]
    xc = xin_scr[...] + gw[:, 0:1] * yrows[0:w] + gw[:, 1:2] * yrows[w:2 * w]
    issue_next(0)
    issue_next(1)
    hn = _rms(xc, nmix_ref[...]).astype(bf16)
    issue_next(2)
    z = jnp.dot(hn, win_ref[...], preferred_element_type=f32)
    issue_next(3)
    issue_next(4)
    cu = z[:, d:2 * d] * z[:, 2 * d:]
    rows = lax.broadcasted_iota(jnp.int32, (w, 1), 0)
    outside = (is_start & (rows < halo)) | (is_end & (rows >= halo + ts))
    cu = jnp.where(outside, 0.0, cu)
    for j in range(n_col):
        cu_scr[j] = cu[:, j * LANES:(j + 1) * LANES]
    issue_next(5)
    taps = wdw_ref.shape[0]
    off = halo - taps // 2
    parts = []
    for j in range(n_col):
        cs = slice(j * LANES, (j + 1) * LANES)
        wj = wdw_ref[:, cs]
        acc = cu_scr[j, pl.ds(off, ts), :] * wj[0:1, :]
        for k in range(1, taps):
            acc = acc + cu_scr[j, pl.ds(off + k, ts), :] * wj[k:k + 1, :]
        parts.append(acc)
    conv = jnp.concatenate(parts, axis=1)
    bv = z[halo:halo + ts, :d] * conv
    issue_next(6)
    m = jnp.dot(bv.astype(bf16), wout_ref[...], preferred_element_type=f32)
    issue_next(7)
    x2 = xc[halo:halo + ts, :] + m
    x2_ref[...] = x2
    h = _rms(x2, nffn_ref[...])
    h_ref[...] = _row_tiles(h.astype(bf16))
    _route(h, wra_ref, wrb_ref, tri_ref, eidx_ref, gate_ref, lrank_ref)


def _sc_layer(x1, y, posw, gw, seq_layout, nmix, win, wdw, wout, nffn, wra, wrb, tri):
    t, d = x1.shape
    ts = TOKEN_TILE
    n_tiles = t // ts
    n_tiles_a, tiles_per_seq_a, tiles_per_seq_b = seq_layout
    hb = ts // SC_HALO
    n_hblk = t // SC_HALO
    w = ts + 2 * SC_HALO
    smem = pltpu.SMEM
    consts = (nmix, win, wdw, wout, nffn, wra, wrb, tri)
    in_specs = [
        pl.BlockSpec((1, 1, 2 * w), lambda i: (0, 0, 0), memory_space=smem),
        pl.BlockSpec((1, 1, 2 * w), lambda i: (jnp.minimum(i + 1, n_tiles - 1), 0, 0), memory_space=smem),
        pl.BlockSpec((SC_HALO, d), lambda i: (jnp.maximum(i * hb - 1, 0), 0)),
        pl.BlockSpec((ts, d), lambda i: (i, 0)),
        pl.BlockSpec((SC_HALO, d), lambda i: (jnp.minimum((i + 1) * hb, n_hblk - 1), 0)),
        pl.BlockSpec((1, w, 2), lambda i: (i, 0, 0)),
        pl.BlockSpec(memory_space=pl.ANY),
    ] + [_const_spec(a.shape) for a in consts]
    out_specs, out_shape = _layer_out(t, d, ts)
    body = functools.partial(_sc_body, n_tiles=n_tiles, n_tiles_a=n_tiles_a,
                             tiles_per_seq_a=tiles_per_seq_a, tiles_per_seq_b=tiles_per_seq_b)
    return pl.pallas_call(
        body,
        grid=(n_tiles,),
        in_specs=in_specs,
        out_specs=out_specs,
        out_shape=out_shape,
        scratch_shapes=[
            pltpu.VMEM((2, 2 * w, d // LANES, LANES), y.dtype),
            pltpu.SemaphoreType.DMA((2,)),
            pltpu.VMEM((w, d), f32),
            pltpu.VMEM((d // LANES, w, LANES), f32),
        ],
        compiler_params=pltpu.CompilerParams(
            dimension_semantics=("arbitrary",), vmem_limit_bytes=VMEM_LIMIT),
        name="sc_layer",
    )(posw, posw, x1, x1, x1, gw, y, *consts)


def _final_body(pos0_ref, posn_ref, x_ref, gt_ref, y_hbm, g_ref, o_ref, gbuf, sem, *, n_tiles):
    ts = x_ref.shape[0]
    i = pl.program_id(0)
    yrows, _ = _combine_gather(i, n_tiles, pos0_ref, posn_ref, y_hbm, gbuf, sem)
    gt = gt_ref[...]
    xc = x_ref[...] + gt[:, 0:1] * yrows[0:ts] + gt[:, 1:2] * yrows[ts:2 * ts]
    o_ref[...] = _rms(xc, g_ref[...])


def _final(x2, y, pos_tiles, gates_t, g, tile_off, n_tiles):
    t, d = x2.shape
    ts = TOKEN_TILE
    smem = pltpu.SMEM
    in_specs = [
        pl.BlockSpec((1, 1, 2 * ts), lambda i: (tile_off, 0, 0), memory_space=smem),
        pl.BlockSpec((1, 1, 2 * ts), lambda i: (tile_off + jnp.minimum(i + 1, n_tiles - 1), 0, 0),
                     memory_space=smem),
        pl.BlockSpec((ts, d), lambda i: (tile_off + i, 0)),
        pl.BlockSpec((ts, 2), lambda i: (tile_off + i, 0)),
        pl.BlockSpec(memory_space=pl.ANY),
        _const_spec(g.shape),
    ]
    return pl.pallas_call(
        functools.partial(_final_body, n_tiles=n_tiles),
        grid=(n_tiles,),
        in_specs=in_specs,
        out_specs=pl.BlockSpec((ts, d), lambda i: (i, 0)),
        out_shape=jax.ShapeDtypeStruct((n_tiles * ts, d), f32),
        scratch_shapes=[pltpu.VMEM((2, 2 * ts, d // LANES, LANES), y.dtype),
                        pltpu.SemaphoreType.DMA((2,))],
        compiler_params=pltpu.CompilerParams(
            dimension_semantics=("arbitrary",), vmem_limit_bytes=VMEM_LIMIT),
        name="final_norm",
    )(pos_tiles, pos_tiles, x2, gates_t, y, g)


def _router_weights(w_rg, w_re):
    d = w_rg.shape[0]
    wt = jnp.concatenate([w_re.T, w_rg.T, jnp.zeros((ROUTER_ROWS - N_EXPERTS - N_GROUPS, d), f32)], axis=0)
    hi = wt.astype(bf16)
    lo = (wt - hi.astype(f32)).astype(bf16)
    return jnp.concatenate([hi, lo], axis=0), hi


def kernel(x_prompt, x_sample, norm_mix, conf_w_pw1, conf_b_pw1, conf_w_dw, conf_b_dw, conf_ln_g,
           conf_ln_b, conf_w_pw2, conf_b_pw2, sc_w_in, sc_w_dw, sc_w_out, norm_ffn, router_group,
           router_expert, w_gate, w_up, w_down, final_norm):
    bp, sp, d = x_prompt.shape
    bs, ss, _ = x_sample.shape
    ts = TOKEN_TILE
    assert norm_mix.shape[0] == 2 and sp % ts == 0 and ss % ts == 0 and d % LANES == 0
    tp, tsamp = bp * sp, bs * ss
    t = tp + tsamp
    seq_layout = (tp // ts, sp // ts, ss // ts)

    row = lambda a: a.reshape(1, -1)
    wra0, wrb0 = _router_weights(router_group[0], router_expert[0])
    wra1, wrb1 = _router_weights(router_group[1], router_expert[1])
    ti = jnp.arange(ts, dtype=jnp.int32)
    tri = (ti[:, None] < ti[None, :]).astype(bf16)

    x1, h1, eidx1, gates1, lrank1 = _conf_layer(
        x_prompt.reshape(tp, d), x_sample.reshape(tsamp, d), seq_layout, row(norm_mix[0]),
        conf_w_pw1[0].astype(bf16), row(conf_b_pw1[0]),
        conf_w_dw[0], row(conf_b_dw[0]), row(conf_ln_g[0]), row(conf_ln_b[0]),
        conf_w_pw2[0].astype(bf16), row(conf_b_pw2[0]), row(norm_ffn[0]), wra0, wrb0, tri)
    plan1 = _route_plan(eidx1, lrank1, ts)
    y1 = _moe(_dispatch(h1, plan1), plan1, 0, w_gate, w_up, w_down)

    tokw = _window_tokens(t, ts, SC_HALO)
    posw1 = jnp.concatenate([plan1["pos"][0][tokw], plan1["pos"][1][tokw]], axis=1)
    posw1 = posw1.reshape(t // ts, 1, 2 * tokw.shape[1])
    gw1 = gates1.T[tokw]
    x2, h2, eidx2, gates2, lrank2 = _sc_layer(
        x1, y1, posw1, gw1, seq_layout, row(norm_mix[1]), sc_w_in[0].astype(bf16), sc_w_dw[0],
        sc_w_out[0].astype(bf16), row(norm_ffn[1]), wra1, wrb1, tri)
    plan2 = _route_plan(eidx2, lrank2, ts)
    y2 = _moe(_dispatch(h2, plan2), plan2, 1, w_gate, w_up, w_down)

    fg = row(final_norm)
    gates2_t = gates2.T
    out_p = _final(x2, y2, plan2["pos_tiles"], gates2_t, fg, 0, tp // ts)
    out_s = _final(x2, y2, plan2["pos_tiles"], gates2_t, fg, tp // ts, tsamp // ts)
    return out_p.reshape(bp, sp, d), out_s.reshape(bs, ss, d)
```

```python
import functools

import jax
import jax.numpy as jnp
from jax import lax
from jax.experimental import pallas as pl
from jax.experimental.pallas import tpu as pltpu

RMS_EPS = 1e-6
LN_EPS = 1e-5
N_GROUPS = 4
EXPERTS_PER_GROUP = 8
N_EXPERTS = N_GROUPS * EXPERTS_PER_GROUP
ROUTER_ROWS = 40
LANES = 128
SUBLANES = 8
TOKEN_TILE = 512
CONF_HALO = 16
SC_HALO = 8
MOE_BLOCK = 512
CONV_ROWS = 256
ROW_DMA_UNROLL = 8
SC_ISSUE_PIECES = 8
VMEM_LIMIT = 56 * 1024 * 1024

f32 = jnp.float32
bf16 = jnp.bfloat16


def _rms(x, g):
    ms = jnp.mean(x * x, axis=-1, keepdims=True)
    return x * lax.rsqrt(ms + RMS_EPS) * g


def _route(h, wra_ref, wrb_ref, tri_ref, eidx_ref, gate_ref, lrank_ref):
    rows = h.shape[0]
    h_hi = h.astype(bf16)
    h_lo = (h - h_hi.astype(f32)).astype(bf16)
    dn = (((1,), (1,)), ((), ()))
    la = lax.dot_general(wra_ref[...], h_hi, dn, preferred_element_type=f32)
    lb = lax.dot_general(wrb_ref[...], h_lo, dn, preferred_element_type=f32)
    lt = la[:ROUTER_ROWS] + la[ROUTER_ROWS:] + lb
    el = lt[:N_EXPERTS]
    gl = lt[N_EXPERTS:N_EXPERTS + N_GROUPS]
    gm = jnp.max(gl, axis=0, keepdims=True)
    gp = jnp.exp(gl - gm)
    gprob = gp / jnp.sum(gp, axis=0, keepdims=True)
    g_top_p = jnp.max(gprob, axis=0, keepdims=True)
    iota_g = lax.broadcasted_iota(jnp.int32, (N_GROUPS, rows), 0)
    g_top = jnp.min(jnp.where(gprob == g_top_p, iota_g, N_GROUPS), axis=0, keepdims=True)
    elg = jnp.zeros((EXPERTS_PER_GROUP, rows), f32)
    for g in range(N_GROUPS):
        sl = el[g * EXPERTS_PER_GROUP:(g + 1) * EXPERTS_PER_GROUP]
        elg = elg + jnp.where(g_top == g, sl, 0.0)
    em = jnp.max(elg, axis=0, keepdims=True)
    ep = jnp.exp(elg - em)
    eprob = ep / jnp.sum(ep, axis=0, keepdims=True)
    iota_e = lax.broadcasted_iota(jnp.int32, (EXPERTS_PER_GROUP, rows), 0)
    p1 = jnp.max(eprob, axis=0, keepdims=True)
    i1 = jnp.min(jnp.where(eprob == p1, iota_e, EXPERTS_PER_GROUP), axis=0, keepdims=True)
    rest = jnp.where(iota_e == i1, -1.0, eprob)
    p2 = jnp.max(rest, axis=0, keepdims=True)
    i2 = jnp.min(jnp.where(rest == p2, iota_e, EXPERTS_PER_GROUP), axis=0, keepdims=True)
    den = p1 + p2
    e1 = g_top * EXPERTS_PER_GROUP + i1
    e2 = g_top * EXPERTS_PER_GROUP + i2
    eidx_ref[0:1, :] = e1
    eidx_ref[1:2, :] = e2
    gate_ref[0:1, :] = g_top_p * (p1 / den)
    gate_ref[1:2, :] = g_top_p * (p2 / den)

    iota_x = lax.broadcasted_iota(jnp.int32, (N_EXPERTS, rows), 0)
    oh = jnp.concatenate([iota_x == e1, iota_x == e2], axis=0).astype(f32)
    before = jnp.dot(oh.astype(bf16), tri_ref[...], preferred_element_type=f32)
    cnt1 = jnp.sum(oh[:N_EXPERTS], axis=1, keepdims=True)
    r1 = jnp.sum(oh[:N_EXPERTS] * before[:N_EXPERTS], axis=0, keepdims=True)
    r2 = jnp.sum(oh[N_EXPERTS:] * (before[N_EXPERTS:] + cnt1), axis=0, keepdims=True)
    lrank_ref[0:1, :] = r1.astype(jnp.int32)
    lrank_ref[1:2, :] = r2.astype(jnp.int32)


def _seq_edges(i, n_tiles_a, tiles_per_seq_a, tiles_per_seq_b):
    in_a = i < n_tiles_a
    per = jnp.where(in_a, tiles_per_seq_a, tiles_per_seq_b)
    j = jnp.where(in_a, i, i - n_tiles_a) % per
    return j == 0, j == per - 1


def _const_spec(shape):
    nd = len(shape)
    return pl.BlockSpec(shape, lambda i: (0,) * nd, pipeline_mode=pl.Buffered(1))


def _row_tiles(x):
    return x.reshape(x.shape[0], x.shape[1] // LANES, LANES)


def _layer_out(t, d, ts):
    out_specs = [
        pl.BlockSpec((ts, d), lambda i: (i, 0)),
        pl.BlockSpec((ts, d // LANES, LANES), lambda i: (i, 0, 0)),
        pl.BlockSpec((2, ts), lambda i: (0, i)),
        pl.BlockSpec((2, ts), lambda i: (0, i)),
        pl.BlockSpec((2, ts), lambda i: (0, i)),
    ]
    out_shape = [
        jax.ShapeDtypeStruct((t, d), f32),
        jax.ShapeDtypeStruct((t, d // LANES, LANES), bf16),
        jax.ShapeDtypeStruct((2, t), jnp.int32),
        jax.ShapeDtypeStruct((2, t), f32),
        jax.ShapeDtypeStruct((2, t), jnp.int32),
    ]
    return out_specs, out_shape


def _conf_body(aprev_ref, amain_ref, anext_ref, bprev_ref, bmain_ref, bnext_ref,
               nmix_ref, w1_ref, b1_ref, wdw_ref, bdw_ref,
               lng_ref, lnb_ref, w2_ref, b2_ref, nffn_ref, wra_ref, wrb_ref, tri_ref,
               x1_ref, h_ref, eidx_ref, gate_ref, lrank_ref, xn_scr, v_scr, u_scr,
               *, n_tiles_a, tiles_per_seq_a, tiles_per_seq_b):
    ts, d = amain_ref.shape
    halo = CONF_HALO
    w = ts + 2 * halo
    n_col = d // LANES
    taps = wdw_ref.shape[0]
    i = pl.program_id(0)
    is_start, is_end = _seq_edges(i, n_tiles_a, tiles_per_seq_a, tiles_per_seq_b)
    in_a = i < n_tiles_a
    xmain = jnp.where(in_a, amain_ref[...], bmain_ref[...])

    g = nmix_ref[...]
    xn_scr[0:halo, :] = _rms(jnp.where(in_a, aprev_ref[...], bprev_ref[...]), g).astype(bf16)
    xn_scr[halo:halo + ts, :] = _rms(xmain, g).astype(bf16)
    xn_scr[halo + ts:w, :] = _rms(jnp.where(in_a, anext_ref[...], bnext_ref[...]), g).astype(bf16)
    z = jnp.dot(xn_scr[...], w1_ref[...], preferred_element_type=f32) + b1_ref[...]
    v = z[:, :d] * jax.nn.sigmoid(z[:, d:])
    rows = lax.broadcasted_iota(jnp.int32, (w, 1), 0)
    outside = (is_start & (rows < halo)) | (is_end & (rows >= halo + ts))
    v = jnp.where(outside, 0.0, v)
    for j in range(n_col):
        v_scr[j] = v[:, j * LANES:(j + 1) * LANES]

    off = halo - taps // 2
    for j in range(n_col):
        cs = slice(j * LANES, (j + 1) * LANES)
        for c in range(ts // CONV_ROWS):
            acc = v_scr[j, pl.ds(c * CONV_ROWS + off, CONV_ROWS), :] * wdw_ref[0:1, cs]
            for k in range(1, taps):
                acc = acc + v_scr[j, pl.ds(c * CONV_ROWS + off + k, CONV_ROWS), :] * wdw_ref[k:k + 1, cs]
            u_scr[pl.ds(c * CONV_ROWS, CONV_ROWS), cs] = acc + bdw_ref[:, cs]

    u = u_scr[...]
    mu = jnp.mean(u, axis=-1, keepdims=True)
    uc = u - mu
    var = jnp.mean(uc * uc, axis=-1, keepdims=True)
    y = uc * lax.rsqrt(var + LN_EPS) * lng_ref[...] + lnb_ref[...]
    s = y * jax.nn.sigmoid(y)
    m = jnp.dot(s.astype(bf16), w2_ref[...], preferred_element_type=f32) + b2_ref[...]
    x1 = jnp.where(in_a, amain_ref[...], bmain_ref[...]) + m
    x1_ref[...] = x1
    h = _rms(x1, nffn_ref[...])
    h_ref[...] = _row_tiles(h.astype(bf16))
    _route(h, wra_ref, wrb_ref, tri_ref, eidx_ref, gate_ref, lrank_ref)


def _halo_specs(n_rows, d, ts, halo, tile_of_step):
    hb = ts // halo
    last = n_rows // halo - 1
    return [
        pl.BlockSpec((halo, d), lambda i: (jnp.maximum(tile_of_step(i) * hb - 1, 0), 0)),
        pl.BlockSpec((ts, d), lambda i: (tile_of_step(i), 0)),
        pl.BlockSpec((halo, d), lambda i: (jnp.minimum((tile_of_step(i) + 1) * hb, last), 0)),
    ]


def _conf_layer(xa, xb, seq_layout, nmix, w1, b1, wdw, bdw, lng, lnb, w2, b2, nffn, wra, wrb, tri):
    d = xa.shape[1]
    t = xa.shape[0] + xb.shape[0]
    ts = TOKEN_TILE
    n_tiles = t // ts
    n_tiles_a, tiles_per_seq_a, tiles_per_seq_b = seq_layout
    consts = (nmix, w1, b1, wdw, bdw, lng, lnb, w2, b2, nffn, wra, wrb, tri)
    in_specs = (
        _halo_specs(xa.shape[0], d, ts, CONF_HALO, lambda i: jnp.minimum(i, n_tiles_a - 1))
        + _halo_specs(xb.shape[0], d, ts, CONF_HALO, lambda i: jnp.maximum(i - n_tiles_a, 0))
        + [_const_spec(a.shape) for a in consts])
    out_specs, out_shape = _layer_out(t, d, ts)
    w = ts + 2 * CONF_HALO
    body = functools.partial(_conf_body, n_tiles_a=n_tiles_a, tiles_per_seq_a=tiles_per_seq_a,
                             tiles_per_seq_b=tiles_per_seq_b)
    return pl.pallas_call(
        body,
        grid=(n_tiles,),
        in_specs=in_specs,
        out_specs=out_specs,
        out_shape=out_shape,
        scratch_shapes=[
            pltpu.VMEM((w, d), bf16),
            pltpu.VMEM((d // LANES, w, LANES), f32),
            pltpu.VMEM((ts, d), f32),
        ],
        compiler_params=pltpu.CompilerParams(
            dimension_semantics=("arbitrary",), vmem_limit_bytes=VMEM_LIMIT),
        name="conf_layer",
    )(xa, xa, xa, xb, xb, xb, *consts)


def _route_plan(eidx, lrank, ts):
    t = eidx.shape[1]
    n_tiles = t // ts
    bm = MOE_BLOCK
    length = ((2 * t + N_EXPERTS * bm + bm - 1) // bm) * bm
    nb = length // bm
    experts = jnp.arange(N_EXPERTS, dtype=jnp.int32)
    tiles = jnp.arange(n_tiles, dtype=jnp.int32)
    eidx3 = eidx.reshape(2, n_tiles, ts)
    onehot = eidx3[..., None] == experts
    cnt = jnp.sum(onehot.astype(jnp.int32), axis=(0, 2))
    earlier = tiles[:, None] > tiles[None, :]
    tile_base = jnp.sum(jnp.where(earlier[:, :, None], cnt[None, :, :], 0), axis=1)
    counts = jnp.sum(cnt, axis=0)
    padded = ((counts + bm - 1) // bm) * bm
    pad_end = jnp.sum(jnp.where(experts[:, None] >= experts[None, :], padded[None, :], 0), axis=1)
    pad_start = pad_end - padded
    base = pad_start[None, :] + tile_base
    pos3 = jnp.sum(jnp.where(onehot, base[None, :, None, :], 0), axis=-1) + lrank.reshape(2, n_tiles, ts)
    pos = pos3.reshape(2, t).astype(jnp.int32)
    blk_start = jnp.arange(nb, dtype=jnp.int32) * bm
    block_e = jnp.minimum(jnp.sum((blk_start[:, None] >= pad_end[None, :]).astype(jnp.int32), axis=1),
                          N_EXPERTS - 1).astype(jnp.int32)
    n_used = (pad_end[-1] // bm).astype(jnp.int32).reshape(1)
    pos_tiles = pos.reshape(2, n_tiles, ts).transpose(1, 0, 2).reshape(n_tiles, 1, 2 * ts)
    tail = n_used[0] + experts
    zero_off = jnp.concatenate([pad_end - bm, tail * bm])
    zero_on = jnp.concatenate([padded > 0, tail < nb])
    zero_off = jnp.where(zero_on, zero_off, -1).astype(jnp.int32)
    return dict(pos=pos, pos_tiles=pos_tiles, block_e=block_e, n_used=n_used,
                zero_off=zero_off, length=length)


def _window_tokens(t, ts, halo):
    n_tiles = t // ts
    w = ts + 2 * halo
    tok = jnp.arange(n_tiles, dtype=jnp.int32)[:, None] * ts - halo + jnp.arange(w, dtype=jnp.int32)[None, :]
    return jnp.clip(tok, 0, t - 1)


def _dispatch_body(zoff_ref, pos_ref, h_ref, hs_hbm, zbuf, sem, zsem):
    ts = h_ref.shape[0]
    bm = zbuf.shape[0]
    n_zero = zoff_ref.shape[0]

    @pl.when(pl.program_id(0) == 0)
    def _():
        zbuf[...] = jnp.zeros_like(zbuf)

        def zero_copy(j):
            off = pl.multiple_of(zoff_ref[j], bm)
            return pltpu.make_async_copy(zbuf, hs_hbm.at[pl.ds(off, bm)], zsem)

        def start(j, carry):
            @pl.when(zoff_ref[j] >= 0)
            def _():
                zero_copy(j).start()
            return carry

        def wait(j, carry):
            @pl.when(zoff_ref[j] >= 0)
            def _():
                zero_copy(j).wait()
            return carry

        lax.fori_loop(0, n_zero, start, 0)
        lax.fori_loop(0, n_zero, wait, 0)

    for r in range(ts):
        for k in range(2):
            p = pos_ref[0, 0, k * ts + r]
            pltpu.make_async_copy(h_ref.at[r], hs_hbm.at[p], sem).start(priority=(r + k) % 2)
    for _ in range(2):
        pltpu.make_async_copy(hs_hbm.at[pl.ds(0, ts)], hs_hbm.at[pl.ds(ts, ts)], sem).wait()


def _dispatch(h, plan):
    t, n_sub, n_lane = h.shape
    pos_tiles = plan["pos_tiles"]
    n_tiles = pos_tiles.shape[0]
    ts = t // n_tiles
    grid_spec = pltpu.PrefetchScalarGridSpec(
        num_scalar_prefetch=1,
        grid=(n_tiles,),
        in_specs=[
            pl.BlockSpec((1, 1, 2 * ts), lambda i, z: (i, 0, 0), memory_space=pltpu.SMEM),
            pl.BlockSpec((ts, n_sub, n_lane), lambda i, z: (i, 0, 0)),
        ],
        out_specs=pl.BlockSpec(memory_space=pl.ANY),
        scratch_shapes=[pltpu.VMEM((MOE_BLOCK, n_sub, n_lane), h.dtype), pltpu.SemaphoreType.DMA(()),
                        pltpu.SemaphoreType.DMA(())],
    )
    return pl.pallas_call(
        _dispatch_body,
        grid_spec=grid_spec,
        out_shape=jax.ShapeDtypeStruct((plan["length"], n_sub, n_lane), h.dtype),
        compiler_params=pltpu.CompilerParams(dimension_semantics=("arbitrary",)),
        name="dispatch",
    )(plan["zero_off"], pos_tiles, h)


def _moe_body(be_ref, nused_ref, hs_ref, wg_ref, wu_ref, wd_ref, y_ref, wg_scr, wu_scr, wd_scr):
    b = pl.program_id(0)
    used = b < nused_ref[0]

    @pl.when(used & ((b == 0) | (be_ref[b] != be_ref[jnp.maximum(b - 1, 0)])))
    def _():
        wg_scr[...] = wg_ref[0, 0].astype(bf16)
        wu_scr[...] = wu_ref[0, 0].astype(bf16)
        wd_scr[...] = wd_ref[0, 0].astype(bf16)

    @pl.when(used)
    def _():
        bm, n_sub, n_lane = hs_ref.shape
        x = hs_ref[...].reshape(bm, n_sub * n_lane)
        gt = jnp.dot(x, wg_scr[...], preferred_element_type=f32)
        up = jnp.dot(x, wu_scr[...], preferred_element_type=f32)
        hid = gt * jax.nn.sigmoid(gt) * up
        y = jnp.dot(hid.astype(bf16), wd_scr[...], preferred_element_type=f32)
        y_ref[...] = _row_tiles(y.astype(y_ref.dtype))

    @pl.when(jnp.logical_not(used))
    def _():
        y_ref[...] = jnp.zeros_like(y_ref)


def _moe(hs, plan, layer, w_gate, w_up, w_down):
    length, n_sub, n_lane = hs.shape
    bm = MOE_BLOCK
    nb = length // bm
    _, _, d, d_exp = w_gate.shape
    w_in_spec = pl.BlockSpec((1, 1, d, d_exp), lambda b, be, nu: (layer, be[b], 0, 0))
    grid_spec = pltpu.PrefetchScalarGridSpec(
        num_scalar_prefetch=2,
        grid=(nb,),
        in_specs=[
            pl.BlockSpec((bm, n_sub, n_lane), lambda b, be, nu: (jnp.minimum(b, nu[0] - 1), 0, 0)),
            w_in_spec,
            w_in_spec,
            pl.BlockSpec((1, 1, d_exp, d), lambda b, be, nu: (layer, be[b], 0, 0)),
        ],
        out_specs=pl.BlockSpec((bm, n_sub, n_lane), lambda b, be, nu: (b, 0, 0)),
        scratch_shapes=[pltpu.VMEM((d, d_exp), bf16), pltpu.VMEM((d, d_exp), bf16),
                        pltpu.VMEM((d_exp, d), bf16)],
    )
    return pl.pallas_call(
        _moe_body,
        grid_spec=grid_spec,
        out_shape=jax.ShapeDtypeStruct((length, n_sub, n_lane), bf16),
        compiler_params=pltpu.CompilerParams(
            dimension_semantics=("arbitrary",), vmem_limit_bytes=VMEM_LIMIT),
        name="moe",
    )(plan["block_e"], plan["n_used"], hs, w_gate, w_up, w_down)


def _combine_gather(i, n_tiles, pos0_ref, posn_ref, y_hbm, gbuf, sem, spread=None):
    _, n_rows, n_sub, n_lane = gbuf.shape

    def issue(idx_ref, slot):
        def body(r8, carry):
            for u in range(ROW_DMA_UNROLL):
                r = r8 * ROW_DMA_UNROLL + u
                p = idx_ref[0, 0, r]
                pltpu.make_async_copy(y_hbm.at[p], gbuf.at[slot, r], sem.at[slot]).start(priority=u % 2)
            return carry
        lax.fori_loop(0, n_rows // ROW_DMA_UNROLL, body, 0)

    @pl.when(i == 0)
    def _():
        issue(pos0_ref, 0)

    slot = i % 2
    nslot = 1 - slot

    def wait(s):
        pltpu.make_async_copy(gbuf.at[s], gbuf.at[s], sem.at[s]).wait()

    if spread is None:
        @pl.when(i + 1 < n_tiles)
        def _():
            issue(posn_ref, nslot)
        issue_next = None
    else:
        piece = -(-n_rows // spread)

        def issue_next(q):
            for r in range(q * piece, min((q + 1) * piece, n_rows)):
                p = posn_ref[0, 0, r]
                pltpu.make_async_copy(y_hbm.at[p], gbuf.at[nslot, r], sem.at[nslot]).start(priority=r % 2)
            if q == spread - 1:
                @pl.when(i == n_tiles - 1)
                def _():
                    wait(nslot)

    wait(slot)
    return gbuf[slot].reshape(n_rows, n_sub * n_lane).astype(f32), issue_next


def _sc_body(pos0_ref, posn_ref, xprev_ref, xmain_ref, xnext_ref, gw_ref, y_hbm, nmix_ref, win_ref,
             wdw_ref, wout_ref, nffn_ref, wra_ref, wrb_ref, tri_ref,
             x2_ref, h_ref, eidx_ref, gate_ref, lrank_ref, gbuf, sem, xin_scr, cu_scr,
             *, n_tiles, n_tiles_a, tiles_per_seq_a, tiles_per_seq_b):
    ts, d = xmain_ref.shape
    halo = SC_HALO
    w = ts + 2 * halo
    n_col = d // LANES
    i = pl.program_id(0)
    is_start, is_end = _seq_edges(i, n_tiles_a, tiles_per_seq_a, tiles_per_seq_b)
    yrows, issue_next = _combine_gather(i, n_tiles, pos0_ref, posn_ref, y_hbm, gbuf, sem,
                                        spread=SC_ISSUE_PIECES)

    xin_scr[0:halo, :] = xprev_ref[...]
    xin_scr[halo:halo + ts, :] = xmain_ref[...]
    xin_scr[halo + ts:w, :] = xnext_ref[...]
    gw = gw_ref[0]
    xc = xin_scr[...] + gw[:, 0:1] * yrows[0:w] + gw[:, 1:2] * yrows[w:2 * w]
    issue_next(0)
    issue_next(1)
    hn = _rms(xc, nmix_ref[...]).astype(bf16)
    issue_next(2)
    z = jnp.dot(hn, win_ref[...], preferred_element_type=f32)
    issue_next(3)
    issue_next(4)
    cu = z[:, d:2 * d] * z[:, 2 * d:]
    rows = lax.broadcasted_iota(jnp.int32, (w, 1), 0)
    outside = (is_start & (rows < halo)) | (is_end & (rows >= halo + ts))
    cu = jnp.where(outside, 0.0, cu)
    for j in range(n_col):
        cu_scr[j] = cu[:, j * LANES:(j + 1) * LANES]
    issue_next(5)
    taps = wdw_ref.shape[0]
    off = halo - taps // 2
    parts = []
    for j in range(n_col):
        cs = slice(j * LANES, (j + 1) * LANES)
        wj = wdw_ref[:, cs]
        acc = cu_scr[j, pl.ds(off, ts), :] * wj[0:1, :]
        for k in range(1, taps):
            acc = acc + cu_scr[j, pl.ds(off + k, ts), :] * wj[k:k + 1, :]
        parts.append(acc)
    conv = jnp.concatenate(parts, axis=1)
    bv = z[halo:halo + ts, :d] * conv
    issue_next(6)
    m = jnp.dot(bv.astype(bf16), wout_ref[...], preferred_element_type=f32)
    issue_next(7)
    x2 = xc[halo:halo + ts, :] + m
    x2_ref[...] = x2
    h = _rms(x2, nffn_ref[...])
    h_ref[...] = _row_tiles(h.astype(bf16))
    _route(h, wra_ref, wrb_ref, tri_ref, eidx_ref, gate_ref, lrank_ref)


def _sc_layer(x1, y, posw, gw, seq_layout, nmix, win, wdw, wout, nffn, wra, wrb, tri):
    t, d = x1.shape
    ts = TOKEN_TILE
    n_tiles = t // ts
    n_tiles_a, tiles_per_seq_a, tiles_per_seq_b = seq_layout
    hb = ts // SC_HALO
    n_hblk = t // SC_HALO
    w = ts + 2 * SC_HALO
    smem = pltpu.SMEM
    consts = (nmix, win, wdw, wout, nffn, wra, wrb, tri)
    in_specs = [
        pl.BlockSpec((1, 1, 2 * w), lambda i: (0, 0, 0), memory_space=smem),
        pl.BlockSpec((1, 1, 2 * w), lambda i: (jnp.minimum(i + 1, n_tiles - 1), 0, 0), memory_space=smem),
        pl.BlockSpec((SC_HALO, d), lambda i: (jnp.maximum(i * hb - 1, 0), 0)),
        pl.BlockSpec((ts, d), lambda i: (i, 0)),
        pl.BlockSpec((SC_HALO, d), lambda i: (jnp.minimum((i + 1) * hb, n_hblk - 1), 0)),
        pl.BlockSpec((1, w, 2), lambda i: (i, 0, 0)),
        pl.BlockSpec(memory_space=pl.ANY),
    ] + [_const_spec(a.shape) for a in consts]
    out_specs, out_shape = _layer_out(t, d, ts)
    body = functools.partial(_sc_body, n_tiles=n_tiles, n_tiles_a=n_tiles_a,
                             tiles_per_seq_a=tiles_per_seq_a, tiles_per_seq_b=tiles_per_seq_b)
    return pl.pallas_call(
        body,
        grid=(n_tiles,),
        in_specs=in_specs,
        out_specs=out_specs,
        out_shape=out_shape,
        scratch_shapes=[
            pltpu.VMEM((2, 2 * w, d // LANES, LANES), y.dtype),
            pltpu.SemaphoreType.DMA((2,)),
            pltpu.VMEM((w, d), f32),
            pltpu.VMEM((d // LANES, w, LANES), f32),
        ],
        compiler_params=pltpu.CompilerParams(
            dimension_semantics=("arbitrary",), vmem_limit_bytes=VMEM_LIMIT),
        name="sc_layer",
    )(posw, posw, x1, x1, x1, gw, y, *consts)


def _final_body(pos0_ref, posn_ref, x_ref, gt_ref, y_hbm, g_ref, o_ref, gbuf, sem, *, n_tiles):
    ts = x_ref.shape[0]
    i = pl.program_id(0)
    yrows, _ = _combine_gather(i, n_tiles, pos0_ref, posn_ref, y_hbm, gbuf, sem)
    gt = gt_ref[...]
    xc = x_ref[...] + gt[:, 0:1] * yrows[0:ts] + gt[:, 1:2] * yrows[ts:2 * ts]
    o_ref[...] = _rms(xc, g_ref[...])


def _final(x2, y, pos_tiles, gates_t, g, tile_off, n_tiles):
    t, d = x2.shape
    ts = TOKEN_TILE
    smem = pltpu.SMEM
    in_specs = [
        pl.BlockSpec((1, 1, 2 * ts), lambda i: (tile_off, 0, 0), memory_space=smem),
        pl.BlockSpec((1, 1, 2 * ts), lambda i: (tile_off + jnp.minimum(i + 1, n_tiles - 1), 0, 0),
                     memory_space=smem),
        pl.BlockSpec((ts, d), lambda i: (tile_off + i, 0)),
        pl.BlockSpec((ts, 2), lambda i: (tile_off + i, 0)),
        pl.BlockSpec(memory_space=pl.ANY),
        _const_spec(g.shape),
    ]
    return pl.pallas_call(
        functools.partial(_final_body, n_tiles=n_tiles),
        grid=(n_tiles,),
        in_specs=in_specs,
        out_specs=pl.BlockSpec((ts, d), lambda i: (i, 0)),
        out_shape=jax.ShapeDtypeStruct((n_tiles * ts, d), f32),
        scratch_shapes=[pltpu.VMEM((2, 2 * ts, d // LANES, LANES), y.dtype),
                        pltpu.SemaphoreType.DMA((2,))],
        compiler_params=pltpu.CompilerParams(
            dimension_semantics=("arbitrary",), vmem_limit_bytes=VMEM_LIMIT),
        name="final_norm",
    )(pos_tiles, pos_tiles, x2, gates_t, y, g)


def _router_weights(w_rg, w_re):
    d = w_rg.shape[0]
    wt = jnp.concatenate([w_re.T, w_rg.T, jnp.zeros((ROUTER_ROWS - N_EXPERTS - N_GROUPS, d), f32)], axis=0)
    hi = wt.astype(bf16)
    lo = (wt - hi.astype(f32)).astype(bf16)
    return jnp.concatenate([hi, lo], axis=0), hi


def kernel(x_prompt, x_sample, norm_mix, conf_w_pw1, conf_b_pw1, conf_w_dw, conf_b_dw, conf_ln_g,
           conf_ln_b, conf_w_pw2, conf_b_pw2, sc_w_in, sc_w_dw, sc_w_out, norm_ffn, router_group,
           router_expert, w_gate, w_up, w_down, final_norm):
    bp, sp, d = x_prompt.shape
    bs, ss, _ = x_sample.shape
    ts = TOKEN_TILE
    assert norm_mix.shape[0] == 2 and sp % ts == 0 and ss % ts == 0 and d % LANES == 0
    tp, tsamp = bp * sp, bs * ss
    t = tp + tsamp
    seq_layout = (tp // ts, sp // ts, ss // ts)

    row = lambda a: a.reshape(1, -1)
    wra0, wrb0 = _router_weights(router_group[0], router_expert[0])
    wra1, wrb1 = _router_weights(router_group[1], router_expert[1])
    ti = jnp.arange(ts, dtype=jnp.int32)
    tri = (ti[:, None] < ti[None, :]).astype(bf16)

    x1, h1, eidx1, gates1, lrank1 = _conf_layer(
        x_prompt.reshape(tp, d), x_sample.reshape(tsamp, d), seq_layout, row(norm_mix[0]),
        conf_w_pw1[0].astype(bf16), row(conf_b_pw1[0]),
        conf_w_dw[0], row(conf_b_dw[0]), row(conf_ln_g[0]), row(conf_ln_b[0]),
        conf_w_pw2[0].astype(bf16), row(conf_b_pw2[0]), row(norm_ffn[0]), wra0, wrb0, tri)
    plan1 = _route_plan(eidx1, lrank1, ts)
    y1 = _moe(_dispatch(h1, plan1), plan1, 0, w_gate, w_up, w_down)

    tokw = _window_tokens(t, ts, SC_HALO)
    posw1 = jnp.concatenate([plan1["pos"][0][tokw], plan1["pos"][1][tokw]], axis=1)
    posw1 = posw1.reshape(t // ts, 1, 2 * tokw.shape[1])
    gw1 = gates1.T[tokw]
    x2, h2, eidx2, gates2, lrank2 = _sc_layer(
        x1, y1, posw1, gw1, seq_layout, row(norm_mix[1]), sc_w_in[0].astype(bf16), sc_w_dw[0],
        sc_w_out[0].astype(bf16), row(norm_ffn[1]), wra1, wrb1, tri)
    plan2 = _route_plan(eidx2, lrank2, ts)
    y2 = _moe(_dispatch(h2, plan2), plan2, 1, w_gate, w_up, w_down)

    fg = row(final_norm)
    gates2_t = gates2.T
    out_p = _final(x2, y2, plan2["pos_tiles"], gates2_t, fg, 0, tp // ts)
    out_s = _final(x2, y2, plan2["pos_tiles"], gates2_t, fg, tp // ts, tsamp // ts)
    return out_p.reshape(bp, sp, d), out_s.reshape(bs, ss, d)
```

```python
import functools

import jax
import jax.numpy as jnp
from jax import lax
from jax.experimental import pallas as pl
from jax.experimental.pallas import tpu as pltpu

RMS_EPS = 1e-6
LN_EPS = 1e-5
N_GROUPS = 4
EXPERTS_PER_GROUP = 8
N_EXPERTS = N_GROUPS * EXPERTS_PER_GROUP
ROUTER_ROWS = 40
LANES = 128
SUBLANES = 8
TOKEN_TILE = 512
CONF_HALO = 16
SC_HALO = 8
MOE_BLOCK = 512
CONV_ROWS = 256
ROW_DMA_UNROLL = 8
SC_ISSUE_PIECES = 8
FINAL_ISSUE_PIECES = 4
VMEM_LIMIT = 56 * 1024 * 1024

f32 = jnp.float32
bf16 = jnp.bfloat16


def _rms(x, g):
    ms = jnp.mean(x * x, axis=-1, keepdims=True)
    return x * lax.rsqrt(ms + RMS_EPS) * g


def _route(h, wra_ref, wrb_ref, tri_ref, eidx_ref, gate_ref, lrank_ref):
    rows = h.shape[0]
    h_hi = h.astype(bf16)
    h_lo = (h - h_hi.astype(f32)).astype(bf16)
    dn = (((1,), (1,)), ((), ()))
    la = lax.dot_general(wra_ref[...], h_hi, dn, preferred_element_type=f32)
    lb = lax.dot_general(wrb_ref[...], h_lo, dn, preferred_element_type=f32)
    lt = la[:ROUTER_ROWS] + la[ROUTER_ROWS:] + lb
    el = lt[:N_EXPERTS]
    gl = lt[N_EXPERTS:N_EXPERTS + N_GROUPS]
    gm = jnp.max(gl, axis=0, keepdims=True)
    gp = jnp.exp(gl - gm)
    gprob = gp / jnp.sum(gp, axis=0, keepdims=True)
    g_top_p = jnp.max(gprob, axis=0, keepdims=True)
    iota_g = lax.broadcasted_iota(jnp.int32, (N_GROUPS, rows), 0)
    g_top = jnp.min(jnp.where(gprob == g_top_p, iota_g, N_GROUPS), axis=0, keepdims=True)
    elg = jnp.zeros((EXPERTS_PER_GROUP, rows), f32)
    for g in range(N_GROUPS):
        sl = el[g * EXPERTS_PER_GROUP:(g + 1) * EXPERTS_PER_GROUP]
        elg = elg + jnp.where(g_top == g, sl, 0.0)
    em = jnp.max(elg, axis=0, keepdims=True)
    ep = jnp.exp(elg - em)
    eprob = ep / jnp.sum(ep, axis=0, keepdims=True)
    iota_e = lax.broadcasted_iota(jnp.int32, (EXPERTS_PER_GROUP, rows), 0)
    p1 = jnp.max(eprob, axis=0, keepdims=True)
    i1 = jnp.min(jnp.where(eprob == p1, iota_e, EXPERTS_PER_GROUP), axis=0, keepdims=True)
    rest = jnp.where(iota_e == i1, -1.0, eprob)
    p2 = jnp.max(rest, axis=0, keepdims=True)
    i2 = jnp.min(jnp.where(rest == p2, iota_e, EXPERTS_PER_GROUP), axis=0, keepdims=True)
    den = p1 + p2
    e1 = g_top * EXPERTS_PER_GROUP + i1
    e2 = g_top * EXPERTS_PER_GROUP + i2
    eidx_ref[0:1, :] = e1
    eidx_ref[1:2, :] = e2
    gate_ref[0:1, :] = g_top_p * (p1 / den)
    gate_ref[1:2, :] = g_top_p * (p2 / den)

    iota_x = lax.broadcasted_iota(jnp.int32, (N_EXPERTS, rows), 0)
    oh = jnp.concatenate([iota_x == e1, iota_x == e2], axis=0).astype(f32)
    before = jnp.dot(oh.astype(bf16), tri_ref[...], preferred_element_type=f32)
    cnt1 = jnp.sum(oh[:N_EXPERTS], axis=1, keepdims=True)
    r1 = jnp.sum(oh[:N_EXPERTS] * before[:N_EXPERTS], axis=0, keepdims=True)
    r2 = jnp.sum(oh[N_EXPERTS:] * (before[N_EXPERTS:] + cnt1), axis=0, keepdims=True)
    lrank_ref[0:1, :] = r1.astype(jnp.int32)
    lrank_ref[1:2, :] = r2.astype(jnp.int32)


def _seq_edges(i, n_tiles_a, tiles_per_seq_a, tiles_per_seq_b):
    in_a = i < n_tiles_a
    per = jnp.where(in_a, tiles_per_seq_a, tiles_per_seq_b)
    j = jnp.where(in_a, i, i - n_tiles_a) % per
    return j == 0, j == per - 1


def _const_spec(shape):
    nd = len(shape)
    return pl.BlockSpec(shape, lambda i: (0,) * nd, pipeline_mode=pl.Buffered(1))


def _row_tiles(x):
    return x.reshape(x.shape[0], x.shape[1] // LANES, LANES)


def _layer_out(t, d, ts):
    out_specs = [
        pl.BlockSpec((ts, d), lambda i: (i, 0)),
        pl.BlockSpec((ts, d // LANES, LANES), lambda i: (i, 0, 0)),
        pl.BlockSpec((2, ts), lambda i: (0, i)),
        pl.BlockSpec((2, ts), lambda i: (0, i)),
        pl.BlockSpec((2, ts), lambda i: (0, i)),
    ]
    out_shape = [
        jax.ShapeDtypeStruct((t, d), f32),
        jax.ShapeDtypeStruct((t, d // LANES, LANES), bf16),
        jax.ShapeDtypeStruct((2, t), jnp.int32),
        jax.ShapeDtypeStruct((2, t), f32),
        jax.ShapeDtypeStruct((2, t), jnp.int32),
    ]
    return out_specs, out_shape


def _conf_body(aprev_ref, amain_ref, anext_ref, bprev_ref, bmain_ref, bnext_ref,
               nmix_ref, w1_ref, b1_ref, wdw_ref, bdw_ref,
               lng_ref, lnb_ref, w2_ref, b2_ref, nffn_ref, wra_ref, wrb_ref, tri_ref,
               x1_ref, h_ref, eidx_ref, gate_ref, lrank_ref, xn_scr, v_scr, u_scr,
               *, n_tiles_a, tiles_per_seq_a, tiles_per_seq_b):
    ts, d = amain_ref.shape
    halo = CONF_HALO
    w = ts + 2 * halo
    n_col = d // LANES
    taps = wdw_ref.shape[0]
    i = pl.program_id(0)
    is_start, is_end = _seq_edges(i, n_tiles_a, tiles_per_seq_a, tiles_per_seq_b)
    in_a = i < n_tiles_a
    xmain = jnp.where(in_a, amain_ref[...], bmain_ref[...])

    g = nmix_ref[...]
    xn_scr[0:halo, :] = _rms(jnp.where(in_a, aprev_ref[...], bprev_ref[...]), g).astype(bf16)
    xn_scr[halo:halo + ts, :] = _rms(xmain, g).astype(bf16)
    xn_scr[halo + ts:w, :] = _rms(jnp.where(in_a, anext_ref[...], bnext_ref[...]), g).astype(bf16)
    z = jnp.dot(xn_scr[...], w1_ref[...], preferred_element_type=f32) + b1_ref[...]
    v = z[:, :d] * jax.nn.sigmoid(z[:, d:])
    rows = lax.broadcasted_iota(jnp.int32, (w, 1), 0)
    outside = (is_start & (rows < halo)) | (is_end & (rows >= halo + ts))
    v = jnp.where(outside, 0.0, v)
    for j in range(n_col):
        v_scr[j] = v[:, j * LANES:(j + 1) * LANES]

    off = halo - taps // 2
    for j in range(n_col):
        cs = slice(j * LANES, (j + 1) * LANES)
        for c in range(ts // CONV_ROWS):
            acc = v_scr[j, pl.ds(c * CONV_ROWS + off, CONV_ROWS), :] * wdw_ref[0:1, cs]
            for k in range(1, taps):
                acc = acc + v_scr[j, pl.ds(c * CONV_ROWS + off + k, CONV_ROWS), :] * wdw_ref[k:k + 1, cs]
            u_scr[pl.ds(c * CONV_ROWS, CONV_ROWS), cs] = acc + bdw_ref[:, cs]

    u = u_scr[...]
    mu = jnp.mean(u, axis=-1, keepdims=True)
    uc = u - mu
    var = jnp.mean(uc * uc, axis=-1, keepdims=True)
    y = uc * lax.rsqrt(var + LN_EPS) * lng_ref[...] + lnb_ref[...]
    s = y * jax.nn.sigmoid(y)
    m = jnp.dot(s.astype(bf16), w2_ref[...], preferred_element_type=f32) + b2_ref[...]
    x1 = jnp.where(in_a, amain_ref[...], bmain_ref[...]) + m
    x1_ref[...] = x1
    h = _rms(x1, nffn_ref[...])
    h_ref[...] = _row_tiles(h.astype(bf16))
    _route(h, wra_ref, wrb_ref, tri_ref, eidx_ref, gate_ref, lrank_ref)


def _halo_specs(n_rows, d, ts, halo, tile_of_step):
    hb = ts // halo
    last = n_rows // halo - 1
    return [
        pl.BlockSpec((halo, d), lambda i: (jnp.maximum(tile_of_step(i) * hb - 1, 0), 0)),
        pl.BlockSpec((ts, d), lambda i: (tile_of_step(i), 0)),
        pl.BlockSpec((halo, d), lambda i: (jnp.minimum((tile_of_step(i) + 1) * hb, last), 0)),
    ]


def _conf_layer(xa, xb, seq_layout, nmix, w1, b1, wdw, bdw, lng, lnb, w2, b2, nffn, wra, wrb, tri):
    d = xa.shape[1]
    t = xa.shape[0] + xb.shape[0]
    ts = TOKEN_TILE
    n_tiles = t // ts
    n_tiles_a, tiles_per_seq_a, tiles_per_seq_b = seq_layout
    consts = (nmix, w1, b1, wdw, bdw, lng, lnb, w2, b2, nffn, wra, wrb, tri)
    in_specs = (
        _halo_specs(xa.shape[0], d, ts, CONF_HALO, lambda i: jnp.minimum(i, n_tiles_a - 1))
        + _halo_specs(xb.shape[0], d, ts, CONF_HALO, lambda i: jnp.maximum(i - n_tiles_a, 0))
        + [_const_spec(a.shape) for a in consts])
    out_specs, out_shape = _layer_out(t, d, ts)
    w = ts + 2 * CONF_HALO
    body = functools.partial(_conf_body, n_tiles_a=n_tiles_a, tiles_per_seq_a=tiles_per_seq_a,
                             tiles_per_seq_b=tiles_per_seq_b)
    return pl.pallas_call(
        body,
        grid=(n_tiles,),
        in_specs=in_specs,
        out_specs=out_specs,
        out_shape=out_shape,
        scratch_shapes=[
            pltpu.VMEM((w, d), bf16),
            pltpu.VMEM((d // LANES, w, LANES), f32),
            pltpu.VMEM((ts, d), f32),
        ],
        compiler_params=pltpu.CompilerParams(
            dimension_semantics=("arbitrary",), vmem_limit_bytes=VMEM_LIMIT),
        name="conf_layer",
    )(xa, xa, xa, xb, xb, xb, *consts)


def _route_plan(eidx, lrank, ts):
    t = eidx.shape[1]
    n_tiles = t // ts
    bm = MOE_BLOCK
    length = ((2 * t + N_EXPERTS * bm + bm - 1) // bm) * bm
    nb = length // bm
    experts = jnp.arange(N_EXPERTS, dtype=jnp.int32)
    tiles = jnp.arange(n_tiles, dtype=jnp.int32)
    eidx3 = eidx.reshape(2, n_tiles, ts)
    onehot = eidx3[..., None] == experts
    cnt = jnp.sum(onehot.astype(jnp.int32), axis=(0, 2))
    earlier = tiles[:, None] > tiles[None, :]
    tile_base = jnp.sum(jnp.where(earlier[:, :, None], cnt[None, :, :], 0), axis=1)
    counts = jnp.sum(cnt, axis=0)
    padded = ((counts + bm - 1) // bm) * bm
    pad_end = jnp.sum(jnp.where(experts[:, None] >= experts[None, :], padded[None, :], 0), axis=1)
    pad_start = pad_end - padded
    base = pad_start[None, :] + tile_base
    pos3 = jnp.sum(jnp.where(onehot, base[None, :, None, :], 0), axis=-1) + lrank.reshape(2, n_tiles, ts)
    pos = pos3.reshape(2, t).astype(jnp.int32)
    blk_start = jnp.arange(nb, dtype=jnp.int32) * bm
    block_e = jnp.minimum(jnp.sum((blk_start[:, None] >= pad_end[None, :]).astype(jnp.int32), axis=1),
                          N_EXPERTS - 1).astype(jnp.int32)
    n_used = (pad_end[-1] // bm).astype(jnp.int32).reshape(1)
    pos_tiles = pos.reshape(2, n_tiles, ts).transpose(1, 0, 2).reshape(n_tiles, 1, 2 * ts)
    tail = n_used[0] + experts
    zero_off = jnp.concatenate([pad_end - bm, tail * bm])
    zero_on = jnp.concatenate([padded > 0, tail < nb])
    zero_off = jnp.where(zero_on, zero_off, -1).astype(jnp.int32)
    return dict(pos=pos, pos_tiles=pos_tiles, block_e=block_e, n_used=n_used,
                zero_off=zero_off, length=length)


def _window_tokens(t, ts, halo):
    n_tiles = t // ts
    w = ts + 2 * halo
    tok = jnp.arange(n_tiles, dtype=jnp.int32)[:, None] * ts - halo + jnp.arange(w, dtype=jnp.int32)[None, :]
    return jnp.clip(tok, 0, t - 1)


def _dispatch_body(zoff_ref, pos_ref, h_ref, hs_hbm, zbuf, sem, zsem):
    ts = h_ref.shape[0]
    bm = zbuf.shape[0]
    n_zero = zoff_ref.shape[0]

    @pl.when(pl.program_id(0) == 0)
    def _():
        zbuf[...] = jnp.zeros_like(zbuf)

        def zero_copy(j):
            off = pl.multiple_of(zoff_ref[j], bm)
            return pltpu.make_async_copy(zbuf, hs_hbm.at[pl.ds(off, bm)], zsem)

        def start(j, carry):
            @pl.when(zoff_ref[j] >= 0)
            def _():
                zero_copy(j).start()
            return carry

        def wait(j, carry):
            @pl.when(zoff_ref[j] >= 0)
            def _():
                zero_copy(j).wait()
            return carry

        lax.fori_loop(0, n_zero, start, 0)
        lax.fori_loop(0, n_zero, wait, 0)

    for r in range(ts):
        for k in range(2):
            p = pos_ref[0, 0, k * ts + r]
            pltpu.make_async_copy(h_ref.at[r], hs_hbm.at[p], sem).start(priority=(r + k) % 2)
    for _ in range(2):
        pltpu.make_async_copy(hs_hbm.at[pl.ds(0, ts)], hs_hbm.at[pl.ds(ts, ts)], sem).wait()


def _dispatch(h, plan):
    t, n_sub, n_lane = h.shape
    pos_tiles = plan["pos_tiles"]
    n_tiles = pos_tiles.shape[0]
    ts = t // n_tiles
    grid_spec = pltpu.PrefetchScalarGridSpec(
        num_scalar_prefetch=1,
        grid=(n_tiles,),
        in_specs=[
            pl.BlockSpec((1, 1, 2 * ts), lambda i, z: (i, 0, 0), memory_space=pltpu.SMEM),
            pl.BlockSpec((ts, n_sub, n_lane), lambda i, z: (i, 0, 0)),
        ],
        out_specs=pl.BlockSpec(memory_space=pl.ANY),
        scratch_shapes=[pltpu.VMEM((MOE_BLOCK, n_sub, n_lane), h.dtype), pltpu.SemaphoreType.DMA(()),
                        pltpu.SemaphoreType.DMA(())],
    )
    return pl.pallas_call(
        _dispatch_body,
        grid_spec=grid_spec,
        out_shape=jax.ShapeDtypeStruct((plan["length"], n_sub, n_lane), h.dtype),
        compiler_params=pltpu.CompilerParams(dimension_semantics=("arbitrary",)),
        name="dispatch",
    )(plan["zero_off"], pos_tiles, h)


def _moe_body(be_ref, nused_ref, hs_ref, wg_ref, wu_ref, wd_ref, y_ref, wg_scr, wu_scr, wd_scr):
    b = pl.program_id(0)
    used = b < nused_ref[0]

    @pl.when(used & ((b == 0) | (be_ref[b] != be_ref[jnp.maximum(b - 1, 0)])))
    def _():
        wg_scr[...] = wg_ref[0, 0].astype(bf16)
        wu_scr[...] = wu_ref[0, 0].astype(bf16)
        wd_scr[...] = wd_ref[0, 0].astype(bf16)

    @pl.when(used)
    def _():
        bm, n_sub, n_lane = hs_ref.shape
        x = hs_ref[...].reshape(bm, n_sub * n_lane)
        gt = jnp.dot(x, wg_scr[...], preferred_element_type=f32)
        up = jnp.dot(x, wu_scr[...], preferred_element_type=f32)
        hid = gt * jax.nn.sigmoid(gt) * up
        y = jnp.dot(hid.astype(bf16), wd_scr[...], preferred_element_type=f32)
        y_ref[...] = _row_tiles(y.astype(y_ref.dtype))

    @pl.when(jnp.logical_not(used))
    def _():
        y_ref[...] = jnp.zeros_like(y_ref)


def _moe(hs, plan, layer, w_gate, w_up, w_down, out_dtype):
    length, n_sub, n_lane = hs.shape
    bm = MOE_BLOCK
    nb = length // bm
    _, _, d, d_exp = w_gate.shape
    w_in_spec = pl.BlockSpec((1, 1, d, d_exp), lambda b, be, nu: (layer, be[b], 0, 0))
    grid_spec = pltpu.PrefetchScalarGridSpec(
        num_scalar_prefetch=2,
        grid=(nb,),
        in_specs=[
            pl.BlockSpec((bm, n_sub, n_lane), lambda b, be, nu: (jnp.minimum(b, nu[0] - 1), 0, 0)),
            w_in_spec,
            w_in_spec,
            pl.BlockSpec((1, 1, d_exp, d), lambda b, be, nu: (layer, be[b], 0, 0)),
        ],
        out_specs=pl.BlockSpec((bm, n_sub, n_lane), lambda b, be, nu: (b, 0, 0)),
        scratch_shapes=[pltpu.VMEM((d, d_exp), bf16), pltpu.VMEM((d, d_exp), bf16),
                        pltpu.VMEM((d_exp, d), bf16)],
    )
    return pl.pallas_call(
        _moe_body,
        grid_spec=grid_spec,
        out_shape=jax.ShapeDtypeStruct((length, n_sub, n_lane), out_dtype),
        compiler_params=pltpu.CompilerParams(
            dimension_semantics=("arbitrary",), vmem_limit_bytes=VMEM_LIMIT),
        name="moe",
    )(plan["block_e"], plan["n_used"], hs, w_gate, w_up, w_down)


def _combine_gather(i, n_tiles, pos0_ref, posn_ref, y_hbm, gbuf, sem, spread=None):
    _, n_rows, n_sub, n_lane = gbuf.shape

    def issue(idx_ref, slot):
        def body(r8, carry):
            for u in range(ROW_DMA_UNROLL):
                r = r8 * ROW_DMA_UNROLL + u
                p = idx_ref[0, 0, r]
                pltpu.make_async_copy(y_hbm.at[p], gbuf.at[slot, r], sem.at[slot]).start(priority=u % 2)
            return carry
        lax.fori_loop(0, n_rows // ROW_DMA_UNROLL, body, 0)

    @pl.when(i == 0)
    def _():
        issue(pos0_ref, 0)

    slot = i % 2
    nslot = 1 - slot

    def wait(s):
        pltpu.make_async_copy(gbuf.at[s], gbuf.at[s], sem.at[s]).wait()

    if spread is None:
        @pl.when(i + 1 < n_tiles)
        def _():
            issue(posn_ref, nslot)
        issue_next = None
    else:
        piece = -(-n_rows // spread)

        def issue_next(q):
            for r in range(q * piece, min((q + 1) * piece, n_rows)):
                p = posn_ref[0, 0, r]
                pltpu.make_async_copy(y_hbm.at[p], gbuf.at[nslot, r], sem.at[nslot]).start(priority=r % 2)
            if q == spread - 1:
                @pl.when(i == n_tiles - 1)
                def _():
                    wait(nslot)

    wait(slot)
    return gbuf[slot].reshape(n_rows, n_sub * n_lane).astype(f32), issue_next


def _sc_body(pos0_ref, posn_ref, xprev_ref, xmain_ref, xnext_ref, gw_ref, y_hbm, nmix_ref, win_ref,
             wdw_ref, wout_ref, nffn_ref, wra_ref, wrb_ref, tri_ref,
             x2_ref, h_ref, eidx_ref, gate_ref, lrank_ref, gbuf, sem, xin_scr, cu_scr,
             *, n_tiles, n_tiles_a, tiles_per_seq_a, tiles_per_seq_b):
    ts, d = xmain_ref.shape
    halo = SC_HALO
    w = ts + 2 * halo
    n_col = d // LANES
    i = pl.program_id(0)
    is_start, is_end = _seq_edges(i, n_tiles_a, tiles_per_seq_a, tiles_per_seq_b)
    yrows, issue_next = _combine_gather(i, n_tiles, pos0_ref, posn_ref, y_hbm, gbuf, sem,
                                        spread=SC_ISSUE_PIECES)

    xin_scr[0:halo, :] = xprev_ref[...]
    xin_scr[halo:halo + ts, :] = xmain_ref[...]
    xin_scr[halo + ts:w, :] = xnext_ref[...]
    gw = gw_ref[0]
    xc = xin_scr[...] + gw[:, 0:1] * yrows[0:w] + gw[:, 1:2] * yrows[w:2 * w]
    issue_next(0)
    issue_next(1)
    hn = _rms(xc, nmix_ref[...]).astype(bf16)
    issue_next(2)
    z = jnp.dot(hn, win_ref[...], preferred_element_type=f32)
    issue_next(3)
    issue_next(4)
    cu = z[:, d:2 * d] * z[:, 2 * d:]
    rows = lax.broadcasted_iota(jnp.int32, (w, 1), 0)
    outside = (is_start & (rows < halo)) | (is_end & (rows >= halo + ts))
    cu = jnp.where(outside, 0.0, cu)
    for j in range(n_col):
        cu_scr[j] = cu[:, j * LANES:(j + 1) * LANES]
    issue_next(5)
    taps = wdw_ref.shape[0]
    off = halo - taps // 2
    parts = []
    for j in range(n_col):
        cs = slice(j * LANES, (j + 1) * LANES)
        wj = wdw_ref[:, cs]
        acc = cu_scr[j, pl.ds(off, ts), :] * wj[0:1, :]
        for k in range(1, taps):
            acc = acc + cu_scr[j, pl.ds(off + k, ts), :] * wj[k:k + 1, :]
        parts.append(acc)
    conv = jnp.concatenate(parts, axis=1)
    bv = z[halo:halo + ts, :d] * conv
    issue_next(6)
    m = jnp.dot(bv.astype(bf16), wout_ref[...], preferred_element_type=f32)
    issue_next(7)
    x2 = xc[halo:halo + ts, :] + m
    x2_ref[...] = x2
    h = _rms(x2, nffn_ref[...])
    h_ref[...] = _row_tiles(h.astype(bf16))
    _route(h, wra_ref, wrb_ref, tri_ref, eidx_ref, gate_ref, lrank_ref)


def _sc_layer(x1, y, posw, gw, seq_layout, nmix, win, wdw, wout, nffn, wra, wrb, tri):
    t, d = x1.shape
    ts = TOKEN_TILE
    n_tiles = t // ts
    n_tiles_a, tiles_per_seq_a, tiles_per_seq_b = seq_layout
    hb = ts // SC_HALO
    n_hblk = t // SC_HALO
    w = ts + 2 * SC_HALO
    smem = pltpu.SMEM
    consts = (nmix, win, wdw, wout, nffn, wra, wrb, tri)
    in_specs = [
        pl.BlockSpec((1, 1, 2 * w), lambda i: (0, 0, 0), memory_space=smem),
        pl.BlockSpec((1, 1, 2 * w), lambda i: (jnp.minimum(i + 1, n_tiles - 1), 0, 0), memory_space=smem),
        pl.BlockSpec((SC_HALO, d), lambda i: (jnp.maximum(i * hb - 1, 0), 0)),
        pl.BlockSpec((ts, d), lambda i: (i, 0)),
        pl.BlockSpec((SC_HALO, d), lambda i: (jnp.minimum((i + 1) * hb, n_hblk - 1), 0)),
        pl.BlockSpec((1, w, 2), lambda i: (i, 0, 0)),
        pl.BlockSpec(memory_space=pl.ANY),
    ] + [_const_spec(a.shape) for a in consts]
    out_specs, out_shape = _layer_out(t, d, ts)
    body = functools.partial(_sc_body, n_tiles=n_tiles, n_tiles_a=n_tiles_a,
                             tiles_per_seq_a=tiles_per_seq_a, tiles_per_seq_b=tiles_per_seq_b)
    return pl.pallas_call(
        body,
        grid=(n_tiles,),
        in_specs=in_specs,
        out_specs=out_specs,
        out_shape=out_shape,
        scratch_shapes=[
            pltpu.VMEM((2, 2 * w, d // LANES, LANES), y.dtype),
            pltpu.SemaphoreType.DMA((2,)),
            pltpu.VMEM((w, d), f32),
            pltpu.VMEM((d // LANES, w, LANES), f32),
        ],
        compiler_params=pltpu.CompilerParams(
            dimension_semantics=("arbitrary",), vmem_limit_bytes=VMEM_LIMIT),
        name="sc_layer",
    )(posw, posw, x1, x1, x1, gw, y, *consts)


def _final_body(pos0_ref, posn_ref, x_ref, gt_ref, y_hbm, g_ref, o_ref, gbuf, sem, *, n_tiles):
    ts = x_ref.shape[0]
    i = pl.program_id(0)
    yrows, issue_next = _combine_gather(i, n_tiles, pos0_ref, posn_ref, y_hbm, gbuf, sem,
                                        spread=FINAL_ISSUE_PIECES)
    issue_next(0)
    gt = gt_ref[...]
    xc = x_ref[...] + gt[:, 0:1] * yrows[0:ts] + gt[:, 1:2] * yrows[ts:2 * ts]
    issue_next(1)
    issue_next(2)
    o_ref[...] = _rms(xc, g_ref[...])
    issue_next(3)


def _final(x2, y, pos_tiles, gates_t, g, tile_off, n_tiles):
    t, d = x2.shape
    ts = TOKEN_TILE
    smem = pltpu.SMEM
    in_specs = [
        pl.BlockSpec((1, 1, 2 * ts), lambda i: (tile_off, 0, 0), memory_space=smem),
        pl.BlockSpec((1, 1, 2 * ts), lambda i: (tile_off + jnp.minimum(i + 1, n_tiles - 1), 0, 0),
                     memory_space=smem),
        pl.BlockSpec((ts, d), lambda i: (tile_off + i, 0)),
        pl.BlockSpec((ts, 2), lambda i: (tile_off + i, 0)),
        pl.BlockSpec(memory_space=pl.ANY),
        _const_spec(g.shape),
    ]
    return pl.pallas_call(
        functools.partial(_final_body, n_tiles=n_tiles),
        grid=(n_tiles,),
        in_specs=in_specs,
        out_specs=pl.BlockSpec((ts, d), lambda i: (i, 0)),
        out_shape=jax.ShapeDtypeStruct((n_tiles * ts, d), f32),
        scratch_shapes=[pltpu.VMEM((2, 2 * ts, d // LANES, LANES), y.dtype),
                        pltpu.SemaphoreType.DMA((2,))],
        compiler_params=pltpu.CompilerParams(
            dimension_semantics=("arbitrary",), vmem_limit_bytes=VMEM_LIMIT),
        name="final_norm",
    )(pos_tiles, pos_tiles, x2, gates_t, y, g)


def _router_weights(w_rg, w_re):
    d = w_rg.shape[0]
    wt = jnp.concatenate([w_re.T, w_rg.T, jnp.zeros((ROUTER_ROWS - N_EXPERTS - N_GROUPS, d), f32)], axis=0)
    hi = wt.astype(bf16)
    lo = (wt - hi.astype(f32)).astype(bf16)
    return jnp.concatenate([hi, lo], axis=0), hi


def kernel(x_prompt, x_sample, norm_mix, conf_w_pw1, conf_b_pw1, conf_w_dw, conf_b_dw, conf_ln_g,
           conf_ln_b, conf_w_pw2, conf_b_pw2, sc_w_in, sc_w_dw, sc_w_out, norm_ffn, router_group,
           router_expert, w_gate, w_up, w_down, final_norm):
    bp, sp, d = x_prompt.shape
    bs, ss, _ = x_sample.shape
    ts = TOKEN_TILE
    assert norm_mix.shape[0] == 2 and sp % ts == 0 and ss % ts == 0 and d % LANES == 0
    tp, tsamp = bp * sp, bs * ss
    t = tp + tsamp
    seq_layout = (tp // ts, sp // ts, ss // ts)

    row = lambda a: a.reshape(1, -1)
    wra0, wrb0 = _router_weights(router_group[0], router_expert[0])
    wra1, wrb1 = _router_weights(router_group[1], router_expert[1])
    ti = jnp.arange(ts, dtype=jnp.int32)
    tri = (ti[:, None] < ti[None, :]).astype(bf16)

    x1, h1, eidx1, gates1, lrank1 = _conf_layer(
        x_prompt.reshape(tp, d), x_sample.reshape(tsamp, d), seq_layout, row(norm_mix[0]),
        conf_w_pw1[0].astype(bf16), row(conf_b_pw1[0]),
        conf_w_dw[0], row(conf_b_dw[0]), row(conf_ln_g[0]), row(conf_ln_b[0]),
        conf_w_pw2[0].astype(bf16), row(conf_b_pw2[0]), row(norm_ffn[0]), wra0, wrb0, tri)
    plan1 = _route_plan(eidx1, lrank1, ts)
    y1 = _moe(_dispatch(h1, plan1), plan1, 0, w_gate, w_up, w_down, f32)

    tokw = _window_tokens(t, ts, SC_HALO)
    posw1 = jnp.concatenate([plan1["pos"][0][tokw], plan1["pos"][1][tokw]], axis=1)
    posw1 = posw1.reshape(t // ts, 1, 2 * tokw.shape[1])
    gw1 = gates1.T[tokw]
    x2, h2, eidx2, gates2, lrank2 = _sc_layer(
        x1, y1, posw1, gw1, seq_layout, row(norm_mix[1]), sc_w_in[0].astype(bf16), sc_w_dw[0],
        sc_w_out[0].astype(bf16), row(norm_ffn[1]), wra1, wrb1, tri)
    plan2 = _route_plan(eidx2, lrank2, ts)
    y2 = _moe(_dispatch(h2, plan2), plan2, 1, w_gate, w_up, w_down, bf16)

    fg = row(final_norm)
    gates2_t = gates2.T
    out_p = _final(x2, y2, plan2["pos_tiles"], gates2_t, fg, 0, tp // ts)
    out_s = _final(x2, y2, plan2["pos_tiles"], gates2_t, fg, tp // ts, tsamp // ts)
    return out_p.reshape(bp, sp, d), out_s.reshape(bs, ss, d)
```

```python
import functools

import jax
import jax.numpy as jnp
from jax import lax
from jax.experimental import pallas as pl
from jax.experimental.pallas import tpu as pltpu

RMS_EPS = 1e-6
LN_EPS = 1e-5
N_GROUPS = 4
EXPERTS_PER_GROUP = 8
N_EXPERTS = N_GROUPS * EXPERTS_PER_GROUP
ROUTER_ROWS = 40
LANES = 128
SUBLANES = 8
TOKEN_TILE = 512
CONF_HALO = 16
SC_HALO = 8
MOE_BLOCK = 512
CONV_ROWS = 256
ROW_DMA_UNROLL = 8
SC_ISSUE_PIECES = 8
FINAL_ISSUE_PIECES = 4
VMEM_LIMIT = 56 * 1024 * 1024

f32 = jnp.float32
bf16 = jnp.bfloat16


def _rms(x, g):
    ms = jnp.mean(x * x, axis=-1, keepdims=True)
    return x * lax.rsqrt(ms + RMS_EPS) * g


def _route(h, wra_ref, wrb_ref, tri_ref, eidx_ref, gate_ref, lrank_ref):
    rows = h.shape[0]
    h_hi = h.astype(bf16)
    h_lo = (h - h_hi.astype(f32)).astype(bf16)
    dn = (((1,), (1,)), ((), ()))
    la = lax.dot_general(wra_ref[...], h_hi, dn, preferred_element_type=f32)
    lb = lax.dot_general(wrb_ref[...], h_lo, dn, preferred_element_type=f32)
    lt = la[:ROUTER_ROWS] + la[ROUTER_ROWS:] + lb
    el = lt[:N_EXPERTS]
    gl = lt[N_EXPERTS:N_EXPERTS + N_GROUPS]
    gm = jnp.max(gl, axis=0, keepdims=True)
    gp = jnp.exp(gl - gm)
    gprob = gp / jnp.sum(gp, axis=0, keepdims=True)
    g_top_p = jnp.max(gprob, axis=0, keepdims=True)
    iota_g = lax.broadcasted_iota(jnp.int32, (N_GROUPS, rows), 0)
    g_top = jnp.min(jnp.where(gprob == g_top_p, iota_g, N_GROUPS), axis=0, keepdims=True)
    elg = jnp.zeros((EXPERTS_PER_GROUP, rows), f32)
    for g in range(N_GROUPS):
        sl = el[g * EXPERTS_PER_GROUP:(g + 1) * EXPERTS_PER_GROUP]
        elg = elg + jnp.where(g_top == g, sl, 0.0)
    em = jnp.max(elg, axis=0, keepdims=True)
    ep = jnp.exp(elg - em)
    eprob = ep / jnp.sum(ep, axis=0, keepdims=True)
    iota_e = lax.broadcasted_iota(jnp.int32, (EXPERTS_PER_GROUP, rows), 0)
    p1 = jnp.max(eprob, axis=0, keepdims=True)
    i1 = jnp.min(jnp.where(eprob == p1, iota_e, EXPERTS_PER_GROUP), axis=0, keepdims=True)
    rest = jnp.where(iota_e == i1, -1.0, eprob)
    p2 = jnp.max(rest, axis=0, keepdims=True)
    i2 = jnp.min(jnp.where(rest == p2, iota_e, EXPERTS_PER_GROUP), axis=0, keepdims=True)
    den = p1 + p2
    e1 = g_top * EXPERTS_PER_GROUP + i1
    e2 = g_top * EXPERTS_PER_GROUP + i2
    eidx_ref[0:1, :] = e1
    eidx_ref[1:2, :] = e2
    gate_ref[0:1, :] = g_top_p * (p1 / den)
    gate_ref[1:2, :] = g_top_p * (p2 / den)

    iota_x = lax.broadcasted_iota(jnp.int32, (N_EXPERTS, rows), 0)
    oh = jnp.concatenate([iota_x == e1, iota_x == e2], axis=0).astype(f32)
    before = jnp.dot(oh.astype(bf16), tri_ref[...], preferred_element_type=f32)
    cnt1 = jnp.sum(oh[:N_EXPERTS], axis=1, keepdims=True)
    r1 = jnp.sum(oh[:N_EXPERTS] * before[:N_EXPERTS], axis=0, keepdims=True)
    r2 = jnp.sum(oh[N_EXPERTS:] * (before[N_EXPERTS:] + cnt1), axis=0, keepdims=True)
    lrank_ref[0:1, :] = r1.astype(jnp.int32)
    lrank_ref[1:2, :] = r2.astype(jnp.int32)


def _seq_edges(i, n_tiles_a, tiles_per_seq_a, tiles_per_seq_b):
    in_a = i < n_tiles_a
    per = jnp.where(in_a, tiles_per_seq_a, tiles_per_seq_b)
    j = jnp.where(in_a, i, i - n_tiles_a) % per
    return j == 0, j == per - 1


def _const_spec(shape):
    nd = len(shape)
    return pl.BlockSpec(shape, lambda i: (0,) * nd, pipeline_mode=pl.Buffered(1))


def _row_tiles(x):
    return x.reshape(x.shape[0], x.shape[1] // LANES, LANES)


def _layer_out(t, d, ts):
    out_specs = [
        pl.BlockSpec((ts, d), lambda i: (i, 0)),
        pl.BlockSpec((ts, d // LANES, LANES), lambda i: (i, 0, 0)),
        pl.BlockSpec((2, ts), lambda i: (0, i)),
        pl.BlockSpec((2, ts), lambda i: (0, i)),
        pl.BlockSpec((2, ts), lambda i: (0, i)),
    ]
    out_shape = [
        jax.ShapeDtypeStruct((t, d), f32),
        jax.ShapeDtypeStruct((t, d // LANES, LANES), bf16),
        jax.ShapeDtypeStruct((2, t), jnp.int32),
        jax.ShapeDtypeStruct((2, t), f32),
        jax.ShapeDtypeStruct((2, t), jnp.int32),
    ]
    return out_specs, out_shape


def _conf_body(aprev_ref, amain_ref, anext_ref, bprev_ref, bmain_ref, bnext_ref,
               nmix_ref, w1_ref, b1_ref, wdw_ref, bdw_ref,
               lng_ref, lnb_ref, w2_ref, b2_ref, nffn_ref, wra_ref, wrb_ref, tri_ref,
               x1_ref, h_ref, eidx_ref, gate_ref, lrank_ref, xn_scr, v_scr, u_scr,
               *, n_tiles_a, tiles_per_seq_a, tiles_per_seq_b):
    ts, d = amain_ref.shape
    halo = CONF_HALO
    w = ts + 2 * halo
    n_col = d // LANES
    taps = wdw_ref.shape[0]
    i = pl.program_id(0)
    is_start, is_end = _seq_edges(i, n_tiles_a, tiles_per_seq_a, tiles_per_seq_b)
    in_a = i < n_tiles_a
    xmain = jnp.where(in_a, amain_ref[...], bmain_ref[...])

    g = nmix_ref[...]
    xn_scr[0:halo, :] = _rms(jnp.where(in_a, aprev_ref[...], bprev_ref[...]), g).astype(bf16)
    xn_scr[halo:halo + ts, :] = _rms(xmain, g).astype(bf16)
    xn_scr[halo + ts:w, :] = _rms(jnp.where(in_a, anext_ref[...], bnext_ref[...]), g).astype(bf16)
    z = jnp.dot(xn_scr[...], w1_ref[...], preferred_element_type=f32) + b1_ref[...]
    v = z[:, :d] * jax.nn.sigmoid(z[:, d:])
    rows = lax.broadcasted_iota(jnp.int32, (w, 1), 0)
    outside = (is_start & (rows < halo)) | (is_end & (rows >= halo + ts))
    v = jnp.where(outside, 0.0, v)
    for j in range(n_col):
        v_scr[j] = v[:, j * LANES:(j + 1) * LANES]

    off = halo - taps // 2
    for j in range(n_col):
        cs = slice(j * LANES, (j + 1) * LANES)
        for c in range(ts // CONV_ROWS):
            acc = v_scr[j, pl.ds(c * CONV_ROWS + off, CONV_ROWS), :] * wdw_ref[0:1, cs]
            for k in range(1, taps):
                acc = acc + v_scr[j, pl.ds(c * CONV_ROWS + off + k, CONV_ROWS), :] * wdw_ref[k:k + 1, cs]
            u_scr[pl.ds(c * CONV_ROWS, CONV_ROWS), cs] = acc + bdw_ref[:, cs]

    u = u_scr[...]
    mu = jnp.mean(u, axis=-1, keepdims=True)
    uc = u - mu
    var = jnp.mean(uc * uc, axis=-1, keepdims=True)
    y = uc * lax.rsqrt(var + LN_EPS) * lng_ref[...] + lnb_ref[...]
    s = y * jax.nn.sigmoid(y)
    m = jnp.dot(s.astype(bf16), w2_ref[...], preferred_element_type=f32) + b2_ref[...]
    x1 = jnp.where(in_a, amain_ref[...], bmain_ref[...]) + m
    x1_ref[...] = x1
    h = _rms(x1, nffn_ref[...])
    h_ref[...] = _row_tiles(h.astype(bf16))
    _route(h, wra_ref, wrb_ref, tri_ref, eidx_ref, gate_ref, lrank_ref)


def _halo_specs(n_rows, d, ts, halo, tile_of_step):
    hb = ts // halo
    last = n_rows // halo - 1
    return [
        pl.BlockSpec((halo, d), lambda i: (jnp.maximum(tile_of_step(i) * hb - 1, 0), 0)),
        pl.BlockSpec((ts, d), lambda i: (tile_of_step(i), 0)),
        pl.BlockSpec((halo, d), lambda i: (jnp.minimum((tile_of_step(i) + 1) * hb, last), 0)),
    ]


def _conf_layer(xa, xb, seq_layout, nmix, w1, b1, wdw, bdw, lng, lnb, w2, b2, nffn, wra, wrb, tri):
    d = xa.shape[1]
    t = xa.shape[0] + xb.shape[0]
    ts = TOKEN_TILE
    n_tiles = t // ts
    n_tiles_a, tiles_per_seq_a, tiles_per_seq_b = seq_layout
    consts = (nmix, w1, b1, wdw, bdw, lng, lnb, w2, b2, nffn, wra, wrb, tri)
    in_specs = (
        _halo_specs(xa.shape[0], d, ts, CONF_HALO, lambda i: jnp.minimum(i, n_tiles_a - 1))
        + _halo_specs(xb.shape[0], d, ts, CONF_HALO, lambda i: jnp.maximum(i - n_tiles_a, 0))
        + [_const_spec(a.shape) for a in consts])
    out_specs, out_shape = _layer_out(t, d, ts)
    w = ts + 2 * CONF_HALO
    body = functools.partial(_conf_body, n_tiles_a=n_tiles_a, tiles_per_seq_a=tiles_per_seq_a,
                             tiles_per_seq_b=tiles_per_seq_b)
    return pl.pallas_call(
        body,
        grid=(n_tiles,),
        in_specs=in_specs,
        out_specs=out_specs,
        out_shape=out_shape,
        scratch_shapes=[
            pltpu.VMEM((w, d), bf16),
            pltpu.VMEM((d // LANES, w, LANES), f32),
            pltpu.VMEM((ts, d), f32),
        ],
        compiler_params=pltpu.CompilerParams(
            dimension_semantics=("arbitrary",), vmem_limit_bytes=VMEM_LIMIT),
        name="conf_layer",
    )(xa, xa, xa, xb, xb, xb, *consts)


def _route_plan(eidx, lrank, ts):
    t = eidx.shape[1]
    n_tiles = t // ts
    bm = MOE_BLOCK
    length = ((2 * t + N_EXPERTS * bm + bm - 1) // bm) * bm
    nb = length // bm
    experts = jnp.arange(N_EXPERTS, dtype=jnp.int32)
    tiles = jnp.arange(n_tiles, dtype=jnp.int32)
    eidx3 = eidx.reshape(2, n_tiles, ts)
    onehot = eidx3[..., None] == experts
    cnt = jnp.sum(onehot.astype(jnp.int32), axis=(0, 2))
    earlier = tiles[:, None] > tiles[None, :]
    tile_base = jnp.sum(jnp.where(earlier[:, :, None], cnt[None, :, :], 0), axis=1)
    counts = jnp.sum(cnt, axis=0)
    padded = ((counts + bm - 1) // bm) * bm
    pad_end = jnp.sum(jnp.where(experts[:, None] >= experts[None, :], padded[None, :], 0), axis=1)
    pad_start = pad_end - padded
    base = pad_start[None, :] + tile_base
    pos3 = jnp.sum(jnp.where(onehot, base[None, :, None, :], 0), axis=-1) + lrank.reshape(2, n_tiles, ts)
    pos = pos3.reshape(2, t).astype(jnp.int32)
    blk_start = jnp.arange(nb, dtype=jnp.int32) * bm
    block_e = jnp.minimum(jnp.sum((blk_start[:, None] >= pad_end[None, :]).astype(jnp.int32), axis=1),
                          N_EXPERTS - 1).astype(jnp.int32)
    n_used = (pad_end[-1] // bm).astype(jnp.int32).reshape(1)
    pos_tiles = pos.reshape(2, n_tiles, ts).transpose(1, 0, 2).reshape(n_tiles, 1, 2 * ts)
    tail = n_used[0] + experts
    zero_off = jnp.concatenate([pad_end - bm, tail * bm])
    zero_on = jnp.concatenate([padded > 0, tail < nb])
    zero_off = jnp.where(zero_on, zero_off, -1).astype(jnp.int32)
    return dict(pos=pos, pos_tiles=pos_tiles, block_e=block_e, n_used=n_used,
                zero_off=zero_off, length=length)


def _window_tokens(t, ts, halo):
    n_tiles = t // ts
    w = ts + 2 * halo
    tok = jnp.arange(n_tiles, dtype=jnp.int32)[:, None] * ts - halo + jnp.arange(w, dtype=jnp.int32)[None, :]
    return jnp.clip(tok, 0, t - 1)


def _dispatch_body(zoff_ref, pos_ref, h_ref, hs_hbm, zbuf, sem, zsem):
    ts = h_ref.shape[0]
    bm = zbuf.shape[0]
    n_zero = zoff_ref.shape[0]

    @pl.when(pl.program_id(0) == 0)
    def _():
        zbuf[...] = jnp.zeros_like(zbuf)

        def zero_copy(j):
            off = pl.multiple_of(zoff_ref[j], bm)
            return pltpu.make_async_copy(zbuf, hs_hbm.at[pl.ds(off, bm)], zsem)

        def start(j, carry):
            @pl.when(zoff_ref[j] >= 0)
            def _():
                zero_copy(j).start()
            return carry

        def wait(j, carry):
            @pl.when(zoff_ref[j] >= 0)
            def _():
                zero_copy(j).wait()
            return carry

        lax.fori_loop(0, n_zero, start, 0)
        lax.fori_loop(0, n_zero, wait, 0)

    for r in range(ts):
        for k in range(2):
            p = pos_ref[0, 0, k * ts + r]
            pltpu.make_async_copy(h_ref.at[r], hs_hbm.at[p], sem).start(priority=(r + k) % 2)
    for _ in range(2):
        pltpu.make_async_copy(hs_hbm.at[pl.ds(0, ts)], hs_hbm.at[pl.ds(ts, ts)], sem).wait()


def _dispatch(h, plan):
    t, n_sub, n_lane = h.shape
    pos_tiles = plan["pos_tiles"]
    n_tiles = pos_tiles.shape[0]
    ts = t // n_tiles
    grid_spec = pltpu.PrefetchScalarGridSpec(
        num_scalar_prefetch=1,
        grid=(n_tiles,),
        in_specs=[
            pl.BlockSpec((1, 1, 2 * ts), lambda i, z: (i, 0, 0), memory_space=pltpu.SMEM),
            pl.BlockSpec((ts, n_sub, n_lane), lambda i, z: (i, 0, 0)),
        ],
        out_specs=pl.BlockSpec(memory_space=pl.ANY),
        scratch_shapes=[pltpu.VMEM((MOE_BLOCK, n_sub, n_lane), h.dtype), pltpu.SemaphoreType.DMA(()),
                        pltpu.SemaphoreType.DMA(())],
    )
    return pl.pallas_call(
        _dispatch_body,
        grid_spec=grid_spec,
        out_shape=jax.ShapeDtypeStruct((plan["length"], n_sub, n_lane), h.dtype),
        compiler_params=pltpu.CompilerParams(dimension_semantics=("arbitrary",)),
        name="dispatch",
    )(plan["zero_off"], pos_tiles, h)


def _moe_body(be_ref, nused_ref, hs_ref, wg_ref, wu_ref, wd_ref, y_ref, wg_scr, wu_scr, wd_scr):
    b = pl.program_id(0)
    used = b < nused_ref[0]

    @pl.when(used & ((b == 0) | (be_ref[b] != be_ref[jnp.maximum(b - 1, 0)])))
    def _():
        wg_scr[...] = wg_ref[0, 0].astype(bf16)
        wu_scr[...] = wu_ref[0, 0].astype(bf16)
        wd_scr[...] = wd_ref[0, 0].astype(bf16)

    @pl.when(used)
    def _():
        bm, n_sub, n_lane = hs_ref.shape
        x = hs_ref[...].reshape(bm, n_sub * n_lane)
        gt = jnp.dot(x, wg_scr[...], preferred_element_type=f32)
        up = jnp.dot(x, wu_scr[...], preferred_element_type=f32)
        hid = gt * jax.nn.sigmoid(gt) * up
        y = jnp.dot(hid.astype(bf16), wd_scr[...], preferred_element_type=f32)
        y_ref[...] = _row_tiles(y.astype(y_ref.dtype))

    @pl.when(jnp.logical_not(used))
    def _():
        y_ref[...] = jnp.zeros_like(y_ref)


def _moe(hs, plan, layer, w_gate, w_up, w_down, out_dtype):
    length, n_sub, n_lane = hs.shape
    bm = MOE_BLOCK
    nb = length // bm
    _, _, d, d_exp = w_gate.shape
    w_in_spec = pl.BlockSpec((1, 1, d, d_exp), lambda b, be, nu: (layer, be[b], 0, 0))
    grid_spec = pltpu.PrefetchScalarGridSpec(
        num_scalar_prefetch=2,
        grid=(nb,),
        in_specs=[
            pl.BlockSpec((bm, n_sub, n_lane), lambda b, be, nu: (jnp.minimum(b, nu[0] - 1), 0, 0)),
            w_in_spec,
            w_in_spec,
            pl.BlockSpec((1, 1, d_exp, d), lambda b, be, nu: (layer, be[b], 0, 0)),
        ],
        out_specs=pl.BlockSpec((bm, n_sub, n_lane), lambda b, be, nu: (b, 0, 0)),
        scratch_shapes=[pltpu.VMEM((d, d_exp), bf16), pltpu.VMEM((d, d_exp), bf16),
                        pltpu.VMEM((d_exp, d), bf16)],
    )
    return pl.pallas_call(
        _moe_body,
        grid_spec=grid_spec,
        out_shape=jax.ShapeDtypeStruct((length, n_sub, n_lane), out_dtype),
        compiler_params=pltpu.CompilerParams(
            dimension_semantics=("arbitrary",), vmem_limit_bytes=VMEM_LIMIT),
        name="moe",
    )(plan["block_e"], plan["n_used"], hs, w_gate, w_up, w_down)


def _combine_gather(i, n_tiles, pos0_ref, posn_ref, y_hbm, gbuf, sem, spread=None):
    _, n_rows, n_sub, n_lane = gbuf.shape

    def issue(idx_ref, slot):
        def body(r8, carry):
            for u in range(ROW_DMA_UNROLL):
                r = r8 * ROW_DMA_UNROLL + u
                p = idx_ref[0, 0, r]
                pltpu.make_async_copy(y_hbm.at[p], gbuf.at[slot, r], sem.at[slot]).start(priority=u % 2)
            return carry
        lax.fori_loop(0, n_rows // ROW_DMA_UNROLL, body, 0)

    @pl.when(i == 0)
    def _():
        issue(pos0_ref, 0)

    slot = i % 2
    nslot = 1 - slot

    def wait(s):
        pltpu.make_async_copy(gbuf.at[s], gbuf.at[s], sem.at[s]).wait()

    if spread is None:
        @pl.when(i + 1 < n_tiles)
        def _():
            issue(posn_ref, nslot)
        issue_next = None
    else:
        piece = -(-n_rows // spread)

        def issue_next(q):
            for r in range(q * piece, min((q + 1) * piece, n_rows)):
                p = posn_ref[0, 0, r]
                pltpu.make_async_copy(y_hbm.at[p], gbuf.at[nslot, r], sem.at[nslot]).start(priority=r % 2)
            if q == spread - 1:
                @pl.when(i == n_tiles - 1)
                def _():
                    wait(nslot)

    wait(slot)
    return gbuf[slot].reshape(n_rows, n_sub * n_lane).astype(f32), issue_next


def _sc_body(pos0_ref, posn_ref, xprev_ref, xmain_ref, xnext_ref, gw_ref, y_hbm, nmix_ref, win_ref,
             wdw_ref, wout_ref, nffn_ref, wra_ref, wrb_ref, tri_ref,
             x2_ref, h_ref, eidx_ref, gate_ref, lrank_ref, gbuf, sem, xin_scr, cu_scr,
             *, n_tiles, n_tiles_a, tiles_per_seq_a, tiles_per_seq_b):
    ts, d = xmain_ref.shape
    halo = SC_HALO
    w = ts + 2 * halo
    n_col = d // LANES
    i = pl.program_id(0)
    is_start, is_end = _seq_edges(i, n_tiles_a, tiles_per_seq_a, tiles_per_seq_b)
    yrows, issue_next = _combine_gather(i, n_tiles, pos0_ref, posn_ref, y_hbm, gbuf, sem,
                                        spread=SC_ISSUE_PIECES)

    xin_scr[0:halo, :] = xprev_ref[...]
    xin_scr[halo:halo + ts, :] = xmain_ref[...]
    xin_scr[halo + ts:w, :] = xnext_ref[...]
    gw = gw_ref[0]
    xc = xin_scr[...] + gw[:, 0:1] * yrows[0:w] + gw[:, 1:2] * yrows[w:2 * w]
    issue_next(0)
    hn = _rms(xc, nmix_ref[...]).astype(bf16)
    issue_next(1)
    z = jnp.dot(hn, win_ref[...], preferred_element_type=f32)
    issue_next(2)
    cu = z[:, d:2 * d] * z[:, 2 * d:]
    rows = lax.broadcasted_iota(jnp.int32, (w, 1), 0)
    outside = (is_start & (rows < halo)) | (is_end & (rows >= halo + ts))
    cu = jnp.where(outside, 0.0, cu)
    for j in range(n_col):
        cu_scr[j] = cu[:, j * LANES:(j + 1) * LANES]
    issue_next(3)
    taps = wdw_ref.shape[0]
    off = halo - taps // 2
    parts = []
    for j in range(n_col):
        cs = slice(j * LANES, (j + 1) * LANES)
        wj = wdw_ref[:, cs]
        acc = cu_scr[j, pl.ds(off, ts), :] * wj[0:1, :]
        for k in range(1, taps):
            acc = acc + cu_scr[j, pl.ds(off + k, ts), :] * wj[k:k + 1, :]
        parts.append(acc)
    conv = jnp.concatenate(parts, axis=1)
    bv = z[halo:halo + ts, :d] * conv
    issue_next(4)
    m = jnp.dot(bv.astype(bf16), wout_ref[...], preferred_element_type=f32)
    issue_next(5)
    x2 = xc[halo:halo + ts, :] + m
    x2_ref[...] = x2
    h = _rms(x2, nffn_ref[...])
    h_ref[...] = _row_tiles(h.astype(bf16))
    issue_next(6)
    _route(h, wra_ref, wrb_ref, tri_ref, eidx_ref, gate_ref, lrank_ref)
    issue_next(7)


def _sc_layer(x1, y, posw, gw, seq_layout, nmix, win, wdw, wout, nffn, wra, wrb, tri):
    t, d = x1.shape
    ts = TOKEN_TILE
    n_tiles = t // ts
    n_tiles_a, tiles_per_seq_a, tiles_per_seq_b = seq_layout
    hb = ts // SC_HALO
    n_hblk = t // SC_HALO
    w = ts + 2 * SC_HALO
    smem = pltpu.SMEM
    consts = (nmix, win, wdw, wout, nffn, wra, wrb, tri)
    in_specs = [
        pl.BlockSpec((1, 1, 2 * w), lambda i: (0, 0, 0), memory_space=smem),
        pl.BlockSpec((1, 1, 2 * w), lambda i: (jnp.minimum(i + 1, n_tiles - 1), 0, 0), memory_space=smem),
        pl.BlockSpec((SC_HALO, d), lambda i: (jnp.maximum(i * hb - 1, 0), 0)),
        pl.BlockSpec((ts, d), lambda i: (i, 0)),
        pl.BlockSpec((SC_HALO, d), lambda i: (jnp.minimum((i + 1) * hb, n_hblk - 1), 0)),
        pl.BlockSpec((1, w, 2), lambda i: (i, 0, 0)),
        pl.BlockSpec(memory_space=pl.ANY),
    ] + [_const_spec(a.shape) for a in consts]
    out_specs, out_shape = _layer_out(t, d, ts)
    body = functools.partial(_sc_body, n_tiles=n_tiles, n_tiles_a=n_tiles_a,
                             tiles_per_seq_a=tiles_per_seq_a, tiles_per_seq_b=tiles_per_seq_b)
    return pl.pallas_call(
        body,
        grid=(n_tiles,),
        in_specs=in_specs,
        out_specs=out_specs,
        out_shape=out_shape,
        scratch_shapes=[
            pltpu.VMEM((2, 2 * w, d // LANES, LANES), y.dtype),
            pltpu.SemaphoreType.DMA((2,)),
            pltpu.VMEM((w, d), f32),
            pltpu.VMEM((d // LANES, w, LANES), f32),
        ],
        compiler_params=pltpu.CompilerParams(
            dimension_semantics=("arbitrary",), vmem_limit_bytes=VMEM_LIMIT),
        name="sc_layer",
    )(posw, posw, x1, x1, x1, gw, y, *consts)


def _final_body(pos0_ref, posn_ref, x_ref, gt_ref, y_hbm, g_ref, o_ref, gbuf, sem, *, n_tiles):
    ts = x_ref.shape[0]
    i = pl.program_id(0)
    yrows, issue_next = _combine_gather(i, n_tiles, pos0_ref, posn_ref, y_hbm, gbuf, sem,
                                        spread=FINAL_ISSUE_PIECES)
    issue_next(0)
    gt = gt_ref[...]
    xc = x_ref[...] + gt[:, 0:1] * yrows[0:ts] + gt[:, 1:2] * yrows[ts:2 * ts]
    issue_next(1)
    issue_next(2)
    o_ref[...] = _rms(xc, g_ref[...])
    issue_next(3)


def _final(x2, y, pos_tiles, gates_t, g, tile_off, n_tiles):
    t, d = x2.shape
    ts = TOKEN_TILE
    smem = pltpu.SMEM
    in_specs = [
        pl.BlockSpec((1, 1, 2 * ts), lambda i: (tile_off, 0, 0), memory_space=smem),
        pl.BlockSpec((1, 1, 2 * ts), lambda i: (tile_off + jnp.minimum(i + 1, n_tiles - 1), 0, 0),
                     memory_space=smem),
        pl.BlockSpec((ts, d), lambda i: (tile_off + i, 0)),
        pl.BlockSpec((ts, 2), lambda i: (tile_off + i, 0)),
        pl.BlockSpec(memory_space=pl.ANY),
        _const_spec(g.shape),
    ]
    return pl.pallas_call(
        functools.partial(_final_body, n_tiles=n_tiles),
        grid=(n_tiles,),
        in_specs=in_specs,
        out_specs=pl.BlockSpec((ts, d), lambda i: (i, 0)),
        out_shape=jax.ShapeDtypeStruct((n_tiles * ts, d), f32),
        scratch_shapes=[pltpu.VMEM((2, 2 * ts, d // LANES, LANES), y.dtype),
                        pltpu.SemaphoreType.DMA((2,))],
        compiler_params=pltpu.CompilerParams(
            dimension_semantics=("arbitrary",), vmem_limit_bytes=VMEM_LIMIT),
        name="final_norm",
    )(pos_tiles, pos_tiles, x2, gates_t, y, g)


def _router_weights(w_rg, w_re):
    d = w_rg.shape[0]
    wt = jnp.concatenate([w_re.T, w_rg.T, jnp.zeros((ROUTER_ROWS - N_EXPERTS - N_GROUPS, d), f32)], axis=0)
    hi = wt.astype(bf16)
    lo = (wt - hi.astype(f32)).astype(bf16)
    return jnp.concatenate([hi, lo], axis=0), hi


def kernel(x_prompt, x_sample, norm_mix, conf_w_pw1, conf_b_pw1, conf_w_dw, conf_b_dw, conf_ln_g,
           conf_ln_b, conf_w_pw2, conf_b_pw2, sc_w_in, sc_w_dw, sc_w_out, norm_ffn, router_group,
           router_expert, w_gate, w_up, w_down, final_norm):
    bp, sp, d = x_prompt.shape
    bs, ss, _ = x_sample.shape
    ts = TOKEN_TILE
    assert norm_mix.shape[0] == 2 and sp % ts == 0 and ss % ts == 0 and d % LANES == 0
    tp, tsamp = bp * sp, bs * ss
    t = tp + tsamp
    seq_layout = (tp // ts, sp // ts, ss // ts)

    row = lambda a: a.reshape(1, -1)
    wra0, wrb0 = _router_weights(router_group[0], router_expert[0])
    wra1, wrb1 = _router_weights(router_group[1], router_expert[1])
    ti = jnp.arange(ts, dtype=jnp.int32)
    tri = (ti[:, None] < ti[None, :]).astype(bf16)

    x1, h1, eidx1, gates1, lrank1 = _conf_layer(
        x_prompt.reshape(tp, d), x_sample.reshape(tsamp, d), seq_layout, row(norm_mix[0]),
        conf_w_pw1[0].astype(bf16), row(conf_b_pw1[0]),
        conf_w_dw[0], row(conf_b_dw[0]), row(conf_ln_g[0]), row(conf_ln_b[0]),
        conf_w_pw2[0].astype(bf16), row(conf_b_pw2[0]), row(norm_ffn[0]), wra0, wrb0, tri)
    plan1 = _route_plan(eidx1, lrank1, ts)
    y1 = _moe(_dispatch(h1, plan1), plan1, 0, w_gate, w_up, w_down, f32)

    tokw = _window_tokens(t, ts, SC_HALO)
    posw1 = jnp.concatenate([plan1["pos"][0][tokw], plan1["pos"][1][tokw]], axis=1)
    posw1 = posw1.reshape(t // ts, 1, 2 * tokw.shape[1])
    gw1 = gates1.T[tokw]
    x2, h2, eidx2, gates2, lrank2 = _sc_layer(
        x1, y1, posw1, gw1, seq_layout, row(norm_mix[1]), sc_w_in[0].astype(bf16), sc_w_dw[0],
        sc_w_out[0].astype(bf16), row(norm_ffn[1]), wra1, wrb1, tri)
    plan2 = _route_plan(eidx2, lrank2, ts)
    y2 = _moe(_dispatch(h2, plan2), plan2, 1, w_gate, w_up, w_down, bf16)

    fg = row(final_norm)
    gates2_t = gates2.T
    out_p = _final(x2, y2, plan2["pos_tiles"], gates2_t, fg, 0, tp // ts)
    out_s = _final(x2, y2, plan2["pos_tiles"], gates2_t, fg, tp // ts, tsamp // ts)
    return out_p.reshape(bp, sp, d), out_s.reshape(bs, ss, d)
```

```python
import functools

import jax
import jax.numpy as jnp
from jax import lax
from jax.experimental import pallas as pl
from jax.experimental.pallas import tpu as pltpu

RMS_EPS = 1e-6
LN_EPS = 1e-5
N_GROUPS = 4
EXPERTS_PER_GROUP = 8
N_EXPERTS = N_GROUPS * EXPERTS_PER_GROUP
ROUTER_ROWS = 40
LANES = 128
SUBLANES = 8
TOKEN_TILE = 512
CONF_HALO = 16
SC_HALO = 8
MOE_BLOCK = 512
CONV_ROWS = 256
ROW_DMA_UNROLL = 8
SC_ISSUE_PIECES = 8
FINAL_ISSUE_PIECES = 4
DISPATCH_TILES = 2
VMEM_LIMIT = 56 * 1024 * 1024

f32 = jnp.float32
bf16 = jnp.bfloat16


def _rms(x, g):
    ms = jnp.mean(x * x, axis=-1, keepdims=True)
    return x * lax.rsqrt(ms + RMS_EPS) * g


def _route(h, wra_ref, wrb_ref, tri_ref, eidx_ref, gate_ref, lrank_ref):
    rows = h.shape[0]
    h_hi = h.astype(bf16)
    h_lo = (h - h_hi.astype(f32)).astype(bf16)
    dn = (((1,), (1,)), ((), ()))
    la = lax.dot_general(wra_ref[...], h_hi, dn, preferred_element_type=f32)
    lb = lax.dot_general(wrb_ref[...], h_lo, dn, preferred_element_type=f32)
    lt = la[:ROUTER_ROWS] + la[ROUTER_ROWS:] + lb
    el = lt[:N_EXPERTS]
    gl = lt[N_EXPERTS:N_EXPERTS + N_GROUPS]
    gm = jnp.max(gl, axis=0, keepdims=True)
    gp = jnp.exp(gl - gm)
    gprob = gp / jnp.sum(gp, axis=0, keepdims=True)
    g_top_p = jnp.max(gprob, axis=0, keepdims=True)
    iota_g = lax.broadcasted_iota(jnp.int32, (N_GROUPS, rows), 0)
    g_top = jnp.min(jnp.where(gprob == g_top_p, iota_g, N_GROUPS), axis=0, keepdims=True)
    elg = jnp.zeros((EXPERTS_PER_GROUP, rows), f32)
    for g in range(N_GROUPS):
        sl = el[g * EXPERTS_PER_GROUP:(g + 1) * EXPERTS_PER_GROUP]
        elg = elg + jnp.where(g_top == g, sl, 0.0)
    em = jnp.max(elg, axis=0, keepdims=True)
    ep = jnp.exp(elg - em)
    eprob = ep / jnp.sum(ep, axis=0, keepdims=True)
    iota_e = lax.broadcasted_iota(jnp.int32, (EXPERTS_PER_GROUP, rows), 0)
    p1 = jnp.max(eprob, axis=0, keepdims=True)
    i1 = jnp.min(jnp.where(eprob == p1, iota_e, EXPERTS_PER_GROUP), axis=0, keepdims=True)
    rest = jnp.where(iota_e == i1, -1.0, eprob)
    p2 = jnp.max(rest, axis=0, keepdims=True)
    i2 = jnp.min(jnp.where(rest == p2, iota_e, EXPERTS_PER_GROUP), axis=0, keepdims=True)
    den = p1 + p2
    e1 = g_top * EXPERTS_PER_GROUP + i1
    e2 = g_top * EXPERTS_PER_GROUP + i2
    eidx_ref[0:1, :] = e1
    eidx_ref[1:2, :] = e2
    gate_ref[0:1, :] = g_top_p * (p1 / den)
    gate_ref[1:2, :] = g_top_p * (p2 / den)

    iota_x = lax.broadcasted_iota(jnp.int32, (N_EXPERTS, rows), 0)
    oh = jnp.concatenate([iota_x == e1, iota_x == e2], axis=0).astype(f32)
    before = jnp.dot(oh.astype(bf16), tri_ref[...], preferred_element_type=f32)
    cnt1 = jnp.sum(oh[:N_EXPERTS], axis=1, keepdims=True)
    r1 = jnp.sum(oh[:N_EXPERTS] * before[:N_EXPERTS], axis=0, keepdims=True)
    r2 = jnp.sum(oh[N_EXPERTS:] * (before[N_EXPERTS:] + cnt1), axis=0, keepdims=True)
    lrank_ref[0:1, :] = r1.astype(jnp.int32)
    lrank_ref[1:2, :] = r2.astype(jnp.int32)


def _seq_edges(i, n_tiles_a, tiles_per_seq_a, tiles_per_seq_b):
    in_a = i < n_tiles_a
    per = jnp.where(in_a, tiles_per_seq_a, tiles_per_seq_b)
    j = jnp.where(in_a, i, i - n_tiles_a) % per
    return j == 0, j == per - 1


def _const_spec(shape):
    nd = len(shape)
    return pl.BlockSpec(shape, lambda i: (0,) * nd, pipeline_mode=pl.Buffered(1))


def _row_tiles(x):
    return x.reshape(x.shape[0], x.shape[1] // LANES, LANES)


def _layer_out(t, d, ts):
    out_specs = [
        pl.BlockSpec((ts, d), lambda i: (i, 0)),
        pl.BlockSpec((ts, d // LANES, LANES), lambda i: (i, 0, 0)),
        pl.BlockSpec((2, ts), lambda i: (0, i)),
        pl.BlockSpec((2, ts), lambda i: (0, i)),
        pl.BlockSpec((2, ts), lambda i: (0, i)),
    ]
    out_shape = [
        jax.ShapeDtypeStruct((t, d), f32),
        jax.ShapeDtypeStruct((t, d // LANES, LANES), bf16),
        jax.ShapeDtypeStruct((2, t), jnp.int32),
        jax.ShapeDtypeStruct((2, t), f32),
        jax.ShapeDtypeStruct((2, t), jnp.int32),
    ]
    return out_specs, out_shape


def _conf_body(aprev_ref, amain_ref, anext_ref, bprev_ref, bmain_ref, bnext_ref,
               nmix_ref, w1_ref, b1_ref, wdw_ref, bdw_ref,
               lng_ref, lnb_ref, w2_ref, b2_ref, nffn_ref, wra_ref, wrb_ref, tri_ref,
               x1_ref, h_ref, eidx_ref, gate_ref, lrank_ref, xn_scr, v_scr, u_scr,
               *, n_tiles_a, tiles_per_seq_a, tiles_per_seq_b):
    ts, d = amain_ref.shape
    halo = CONF_HALO
    w = ts + 2 * halo
    n_col = d // LANES
    taps = wdw_ref.shape[0]
    i = pl.program_id(0)
    is_start, is_end = _seq_edges(i, n_tiles_a, tiles_per_seq_a, tiles_per_seq_b)
    in_a = i < n_tiles_a
    xmain = jnp.where(in_a, amain_ref[...], bmain_ref[...])

    g = nmix_ref[...]
    xn_scr[0:halo, :] = _rms(jnp.where(in_a, aprev_ref[...], bprev_ref[...]), g).astype(bf16)
    xn_scr[halo:halo + ts, :] = _rms(xmain, g).astype(bf16)
    xn_scr[halo + ts:w, :] = _rms(jnp.where(in_a, anext_ref[...], bnext_ref[...]), g).astype(bf16)
    z = jnp.dot(xn_scr[...], w1_ref[...], preferred_element_type=f32) + b1_ref[...]
    v = z[:, :d] * jax.nn.sigmoid(z[:, d:])
    rows = lax.broadcasted_iota(jnp.int32, (w, 1), 0)
    outside = (is_start & (rows < halo)) | (is_end & (rows >= halo + ts))
    v = jnp.where(outside, 0.0, v)
    for j in range(n_col):
        v_scr[j] = v[:, j * LANES:(j + 1) * LANES]

    off = halo - taps // 2
    for j in range(n_col):
        cs = slice(j * LANES, (j + 1) * LANES)
        for c in range(ts // CONV_ROWS):
            acc = v_scr[j, pl.ds(c * CONV_ROWS + off, CONV_ROWS), :] * wdw_ref[0:1, cs]
            for k in range(1, taps):
                acc = acc + v_scr[j, pl.ds(c * CONV_ROWS + off + k, CONV_ROWS), :] * wdw_ref[k:k + 1, cs]
            u_scr[pl.ds(c * CONV_ROWS, CONV_ROWS), cs] = acc + bdw_ref[:, cs]

    u = u_scr[...]
    mu = jnp.mean(u, axis=-1, keepdims=True)
    uc = u - mu
    var = jnp.mean(uc * uc, axis=-1, keepdims=True)
    y = uc * lax.rsqrt(var + LN_EPS) * lng_ref[...] + lnb_ref[...]
    s = y * jax.nn.sigmoid(y)
    m = jnp.dot(s.astype(bf16), w2_ref[...], preferred_element_type=f32) + b2_ref[...]
    x1 = jnp.where(in_a, amain_ref[...], bmain_ref[...]) + m
    x1_ref[...] = x1
    h = _rms(x1, nffn_ref[...])
    h_ref[...] = _row_tiles(h.astype(bf16))
    _route(h, wra_ref, wrb_ref, tri_ref, eidx_ref, gate_ref, lrank_ref)


def _halo_specs(n_rows, d, ts, halo, tile_of_step):
    hb = ts // halo
    last = n_rows // halo - 1
    return [
        pl.BlockSpec((halo, d), lambda i: (jnp.maximum(tile_of_step(i) * hb - 1, 0), 0)),
        pl.BlockSpec((ts, d), lambda i: (tile_of_step(i), 0)),
        pl.BlockSpec((halo, d), lambda i: (jnp.minimum((tile_of_step(i) + 1) * hb, last), 0)),
    ]


def _conf_layer(xa, xb, seq_layout, nmix, w1, b1, wdw, bdw, lng, lnb, w2, b2, nffn, wra, wrb, tri):
    d = xa.shape[1]
    t = xa.shape[0] + xb.shape[0]
    ts = TOKEN_TILE
    n_tiles = t // ts
    n_tiles_a, tiles_per_seq_a, tiles_per_seq_b = seq_layout
    consts = (nmix, w1, b1, wdw, bdw, lng, lnb, w2, b2, nffn, wra, wrb, tri)
    in_specs = (
        _halo_specs(xa.shape[0], d, ts, CONF_HALO, lambda i: jnp.minimum(i, n_tiles_a - 1))
        + _halo_specs(xb.shape[0], d, ts, CONF_HALO, lambda i: jnp.maximum(i - n_tiles_a, 0))
        + [_const_spec(a.shape) for a in consts])
    out_specs, out_shape = _layer_out(t, d, ts)
    w = ts + 2 * CONF_HALO
    body = functools.partial(_conf_body, n_tiles_a=n_tiles_a, tiles_per_seq_a=tiles_per_seq_a,
                             tiles_per_seq_b=tiles_per_seq_b)
    return pl.pallas_call(
        body,
        grid=(n_tiles,),
        in_specs=in_specs,
        out_specs=out_specs,
        out_shape=out_shape,
        scratch_shapes=[
            pltpu.VMEM((w, d), bf16),
            pltpu.VMEM((d // LANES, w, LANES), f32),
            pltpu.VMEM((ts, d), f32),
        ],
        compiler_params=pltpu.CompilerParams(
            dimension_semantics=("arbitrary",), vmem_limit_bytes=VMEM_LIMIT),
        name="conf_layer",
    )(xa, xa, xa, xb, xb, xb, *consts)


def _route_plan(eidx, lrank, ts):
    t = eidx.shape[1]
    n_tiles = t // ts
    bm = MOE_BLOCK
    length = ((2 * t + N_EXPERTS * bm + bm - 1) // bm) * bm
    nb = length // bm
    experts = jnp.arange(N_EXPERTS, dtype=jnp.int32)
    tiles = jnp.arange(n_tiles, dtype=jnp.int32)
    eidx3 = eidx.reshape(2, n_tiles, ts)
    onehot = eidx3[..., None] == experts
    cnt = jnp.sum(onehot.astype(jnp.int32), axis=(0, 2))
    earlier = tiles[:, None] > tiles[None, :]
    tile_base = jnp.sum(jnp.where(earlier[:, :, None], cnt[None, :, :], 0), axis=1)
    counts = jnp.sum(cnt, axis=0)
    padded = ((counts + bm - 1) // bm) * bm
    pad_end = jnp.sum(jnp.where(experts[:, None] >= experts[None, :], padded[None, :], 0), axis=1)
    pad_start = pad_end - padded
    base = pad_start[None, :] + tile_base
    pos3 = jnp.sum(jnp.where(onehot, base[None, :, None, :], 0), axis=-1) + lrank.reshape(2, n_tiles, ts)
    pos = pos3.reshape(2, t).astype(jnp.int32)
    blk_start = jnp.arange(nb, dtype=jnp.int32) * bm
    block_e = jnp.minimum(jnp.sum((blk_start[:, None] >= pad_end[None, :]).astype(jnp.int32), axis=1),
                          N_EXPERTS - 1).astype(jnp.int32)
    n_used = (pad_end[-1] // bm).astype(jnp.int32).reshape(1)
    pos_tiles = pos.reshape(2, n_tiles, ts).transpose(1, 0, 2).reshape(n_tiles, 1, 2 * ts)
    tail = n_used[0] + experts
    zero_off = jnp.concatenate([pad_end - bm, tail * bm])
    zero_on = jnp.concatenate([padded > 0, tail < nb])
    zero_off = jnp.where(zero_on, zero_off, -1).astype(jnp.int32)
    return dict(pos=pos, pos_tiles=pos_tiles, block_e=block_e, n_used=n_used,
                zero_off=zero_off, length=length)


def _window_tokens(t, ts, halo):
    n_tiles = t // ts
    w = ts + 2 * halo
    tok = jnp.arange(n_tiles, dtype=jnp.int32)[:, None] * ts - halo + jnp.arange(w, dtype=jnp.int32)[None, :]
    return jnp.clip(tok, 0, t - 1)


def _dispatch_body(zoff_ref, pos_ref, h_ref, hs_hbm, zbuf, sem, zsem):
    n_sub_tiles = pos_ref.shape[0]
    ts = h_ref.shape[0] // n_sub_tiles
    bm = zbuf.shape[0]
    n_zero = zoff_ref.shape[0]

    @pl.when(pl.program_id(0) == 0)
    def _():
        zbuf[...] = jnp.zeros_like(zbuf)

        def zero_copy(j):
            off = pl.multiple_of(zoff_ref[j], bm)
            return pltpu.make_async_copy(zbuf, hs_hbm.at[pl.ds(off, bm)], zsem)

        def start(j, carry):
            @pl.when(zoff_ref[j] >= 0)
            def _():
                zero_copy(j).start()
            return carry

        def wait(j, carry):
            @pl.when(zoff_ref[j] >= 0)
            def _():
                zero_copy(j).wait()
            return carry

        lax.fori_loop(0, n_zero, start, 0)
        lax.fori_loop(0, n_zero, wait, 0)

    for q in range(n_sub_tiles):
        for r in range(ts):
            for k in range(2):
                p = pos_ref[q, 0, k * ts + r]
                pltpu.make_async_copy(h_ref.at[q * ts + r], hs_hbm.at[p], sem).start(priority=(r + k) % 2)
    for _ in range(2 * n_sub_tiles):
        pltpu.make_async_copy(hs_hbm.at[pl.ds(0, ts)], hs_hbm.at[pl.ds(ts, ts)], sem).wait()


def _dispatch(h, plan):
    t, n_sub, n_lane = h.shape
    pos_tiles = plan["pos_tiles"]
    n_tiles = pos_tiles.shape[0]
    ts = t // n_tiles
    per_step = DISPATCH_TILES if n_tiles % DISPATCH_TILES == 0 else 1
    grid_spec = pltpu.PrefetchScalarGridSpec(
        num_scalar_prefetch=1,
        grid=(n_tiles // per_step,),
        in_specs=[
            pl.BlockSpec((per_step, 1, 2 * ts), lambda i, z: (i, 0, 0), memory_space=pltpu.SMEM),
            pl.BlockSpec((per_step * ts, n_sub, n_lane), lambda i, z: (i, 0, 0)),
        ],
        out_specs=pl.BlockSpec(memory_space=pl.ANY),
        scratch_shapes=[pltpu.VMEM((MOE_BLOCK, n_sub, n_lane), h.dtype), pltpu.SemaphoreType.DMA(()),
                        pltpu.SemaphoreType.DMA(())],
    )
    return pl.pallas_call(
        _dispatch_body,
        grid_spec=grid_spec,
        out_shape=jax.ShapeDtypeStruct((plan["length"], n_sub, n_lane), h.dtype),
        compiler_params=pltpu.CompilerParams(dimension_semantics=("arbitrary",)),
        name="dispatch",
    )(plan["zero_off"], pos_tiles, h)


def _moe_body(be_ref, nused_ref, hs_ref, wg_ref, wu_ref, wd_ref, y_ref, wg_scr, wu_scr, wd_scr):
    b = pl.program_id(0)
    used = b < nused_ref[0]

    @pl.when(used & ((b == 0) | (be_ref[b] != be_ref[jnp.maximum(b - 1, 0)])))
    def _():
        wg_scr[...] = wg_ref[0, 0].astype(bf16)
        wu_scr[...] = wu_ref[0, 0].astype(bf16)
        wd_scr[...] = wd_ref[0, 0].astype(bf16)

    @pl.when(used)
    def _():
        bm, n_sub, n_lane = hs_ref.shape
        x = hs_ref[...].reshape(bm, n_sub * n_lane)
        gt = jnp.dot(x, wg_scr[...], preferred_element_type=f32)
        up = jnp.dot(x, wu_scr[...], preferred_element_type=f32)
        hid = gt * jax.nn.sigmoid(gt) * up
        y = jnp.dot(hid.astype(bf16), wd_scr[...], preferred_element_type=f32)
        y_ref[...] = _row_tiles(y.astype(y_ref.dtype))

    @pl.when(jnp.logical_not(used))
    def _():
        y_ref[...] = jnp.zeros_like(y_ref)


def _moe(hs, plan, layer, w_gate, w_up, w_down, out_dtype):
    length, n_sub, n_lane = hs.shape
    bm = MOE_BLOCK
    nb = length // bm
    _, _, d, d_exp = w_gate.shape
    w_in_spec = pl.BlockSpec((1, 1, d, d_exp), lambda b, be, nu: (layer, be[b], 0, 0))
    grid_spec = pltpu.PrefetchScalarGridSpec(
        num_scalar_prefetch=2,
        grid=(nb,),
        in_specs=[
            pl.BlockSpec((bm, n_sub, n_lane), lambda b, be, nu: (jnp.minimum(b, nu[0] - 1), 0, 0)),
            w_in_spec,
            w_in_spec,
            pl.BlockSpec((1, 1, d_exp, d), lambda b, be, nu: (layer, be[b], 0, 0)),
        ],
        out_specs=pl.BlockSpec((bm, n_sub, n_lane), lambda b, be, nu: (b, 0, 0)),
        scratch_shapes=[pltpu.VMEM((d, d_exp), bf16), pltpu.VMEM((d, d_exp), bf16),
                        pltpu.VMEM((d_exp, d), bf16)],
    )
    return pl.pallas_call(
        _moe_body,
        grid_spec=grid_spec,
        out_shape=jax.ShapeDtypeStruct((length, n_sub, n_lane), out_dtype),
        compiler_params=pltpu.CompilerParams(
            dimension_semantics=("arbitrary",), vmem_limit_bytes=VMEM_LIMIT),
        name="moe",
    )(plan["block_e"], plan["n_used"], hs, w_gate, w_up, w_down)


def _combine_gather(i, n_tiles, pos0_ref, posn_ref, y_hbm, gbuf, sem, spread=None):
    _, n_rows, n_sub, n_lane = gbuf.shape

    def issue(idx_ref, slot):
        def body(r8, carry):
            for u in range(ROW_DMA_UNROLL):
                r = r8 * ROW_DMA_UNROLL + u
                p = idx_ref[0, 0, r]
                pltpu.make_async_copy(y_hbm.at[p], gbuf.at[slot, r], sem.at[slot]).start(priority=u % 2)
            return carry
        lax.fori_loop(0, n_rows // ROW_DMA_UNROLL, body, 0)

    @pl.when(i == 0)
    def _():
        issue(pos0_ref, 0)

    slot = i % 2
    nslot = 1 - slot

    def wait(s):
        pltpu.make_async_copy(gbuf.at[s], gbuf.at[s], sem.at[s]).wait()

    if spread is None:
        @pl.when(i + 1 < n_tiles)
        def _():
            issue(posn_ref, nslot)
        issue_next = None
    else:
        piece = -(-n_rows // spread)

        def issue_next(q):
            for r in range(q * piece, min((q + 1) * piece, n_rows)):
                p = posn_ref[0, 0, r]
                pltpu.make_async_copy(y_hbm.at[p], gbuf.at[nslot, r], sem.at[nslot]).start(priority=r % 2)
            if q == spread - 1:
                @pl.when(i == n_tiles - 1)
                def _():
                    wait(nslot)

    wait(slot)
    return gbuf[slot].reshape(n_rows, n_sub * n_lane).astype(f32), issue_next


def _sc_body(pos0_ref, posn_ref, xprev_ref, xmain_ref, xnext_ref, gw_ref, y_hbm, nmix_ref, win_ref,
             wdw_ref, wout_ref, nffn_ref, wra_ref, wrb_ref, tri_ref,
             x2_ref, h_ref, eidx_ref, gate_ref, lrank_ref, gbuf, sem, xin_scr, cu_scr,
             *, n_tiles, n_tiles_a, tiles_per_seq_a, tiles_per_seq_b):
    ts, d = xmain_ref.shape
    halo = SC_HALO
    w = ts + 2 * halo
    n_col = d // LANES
    i = pl.program_id(0)
    is_start, is_end = _seq_edges(i, n_tiles_a, tiles_per_seq_a, tiles_per_seq_b)
    yrows, issue_next = _combine_gather(i, n_tiles, pos0_ref, posn_ref, y_hbm, gbuf, sem,
                                        spread=SC_ISSUE_PIECES)

    xin_scr[0:halo, :] = xprev_ref[...]
    xin_scr[halo:halo + ts, :] = xmain_ref[...]
    xin_scr[halo + ts:w, :] = xnext_ref[...]
    gw = gw_ref[0]
    xc = xin_scr[...] + gw[:, 0:1] * yrows[0:w] + gw[:, 1:2] * yrows[w:2 * w]
    issue_next(0)
    hn = _rms(xc, nmix_ref[...]).astype(bf16)
    issue_next(1)
    z = jnp.dot(hn, win_ref[...], preferred_element_type=f32)
    issue_next(2)
    cu = z[:, d:2 * d] * z[:, 2 * d:]
    rows = lax.broadcasted_iota(jnp.int32, (w, 1), 0)
    outside = (is_start & (rows < halo)) | (is_end & (rows >= halo + ts))
    cu = jnp.where(outside, 0.0, cu)
    for j in range(n_col):
        cu_scr[j] = cu[:, j * LANES:(j + 1) * LANES]
    issue_next(3)
    taps = wdw_ref.shape[0]
    off = halo - taps // 2
    parts = []
    for j in range(n_col):
        cs = slice(j * LANES, (j + 1) * LANES)
        wj = wdw_ref[:, cs]
        acc = cu_scr[j, pl.ds(off, ts), :] * wj[0:1, :]
        for k in range(1, taps):
            acc = acc + cu_scr[j, pl.ds(off + k, ts), :] * wj[k:k + 1, :]
        parts.append(acc)
    conv = jnp.concatenate(parts, axis=1)
    bv = z[halo:halo + ts, :d] * conv
    issue_next(4)
    m = jnp.dot(bv.astype(bf16), wout_ref[...], preferred_element_type=f32)
    issue_next(5)
    x2 = xc[halo:halo + ts, :] + m
    x2_ref[...] = x2
    h = _rms(x2, nffn_ref[...])
    h_ref[...] = _row_tiles(h.astype(bf16))
    issue_next(6)
    _route(h, wra_ref, wrb_ref, tri_ref, eidx_ref, gate_ref, lrank_ref)
    issue_next(7)


def _sc_layer(x1, y, posw, gw, seq_layout, nmix, win, wdw, wout, nffn, wra, wrb, tri):
    t, d = x1.shape
    ts = TOKEN_TILE
    n_tiles = t // ts
    n_tiles_a, tiles_per_seq_a, tiles_per_seq_b = seq_layout
    hb = ts // SC_HALO
    n_hblk = t // SC_HALO
    w = ts + 2 * SC_HALO
    smem = pltpu.SMEM
    consts = (nmix, win, wdw, wout, nffn, wra, wrb, tri)
    in_specs = [
        pl.BlockSpec((1, 1, 2 * w), lambda i: (0, 0, 0), memory_space=smem),
        pl.BlockSpec((1, 1, 2 * w), lambda i: (jnp.minimum(i + 1, n_tiles - 1), 0, 0), memory_space=smem),
        pl.BlockSpec((SC_HALO, d), lambda i: (jnp.maximum(i * hb - 1, 0), 0)),
        pl.BlockSpec((ts, d), lambda i: (i, 0)),
        pl.BlockSpec((SC_HALO, d), lambda i: (jnp.minimum((i + 1) * hb, n_hblk - 1), 0)),
        pl.BlockSpec((1, w, 2), lambda i: (i, 0, 0)),
        pl.BlockSpec(memory_space=pl.ANY),
    ] + [_const_spec(a.shape) for a in consts]
    out_specs, out_shape = _layer_out(t, d, ts)
    body = functools.partial(_sc_body, n_tiles=n_tiles, n_tiles_a=n_tiles_a,
                             tiles_per_seq_a=tiles_per_seq_a, tiles_per_seq_b=tiles_per_seq_b)
    return pl.pallas_call(
        body,
        grid=(n_tiles,),
        in_specs=in_specs,
        out_specs=out_specs,
        out_shape=out_shape,
        scratch_shapes=[
            pltpu.VMEM((2, 2 * w, d // LANES, LANES), y.dtype),
            pltpu.SemaphoreType.DMA((2,)),
            pltpu.VMEM((w, d), f32),
            pltpu.VMEM((d // LANES, w, LANES), f32),
        ],
        compiler_params=pltpu.CompilerParams(
            dimension_semantics=("arbitrary",), vmem_limit_bytes=VMEM_LIMIT),
        name="sc_layer",
    )(posw, posw, x1, x1, x1, gw, y, *consts)


def _final_body(pos0_ref, posn_ref, x_ref, gt_ref, y_hbm, g_ref, o_ref, gbuf, sem, *, n_tiles):
    ts = x_ref.shape[0]
    i = pl.program_id(0)
    yrows, issue_next = _combine_gather(i, n_tiles, pos0_ref, posn_ref, y_hbm, gbuf, sem,
                                        spread=FINAL_ISSUE_PIECES)
    issue_next(0)
    gt = gt_ref[...]
    xc = x_ref[...] + gt[:, 0:1] * yrows[0:ts] + gt[:, 1:2] * yrows[ts:2 * ts]
    issue_next(1)
    issue_next(2)
    o_ref[...] = _rms(xc, g_ref[...])
    issue_next(3)


def _final(x2, y, pos_tiles, gates_t, g, tile_off, n_tiles):
    t, d = x2.shape
    ts = TOKEN_TILE
    smem = pltpu.SMEM
    in_specs = [
        pl.BlockSpec((1, 1, 2 * ts), lambda i: (tile_off, 0, 0), memory_space=smem),
        pl.BlockSpec((1, 1, 2 * ts), lambda i: (tile_off + jnp.minimum(i + 1, n_tiles - 1), 0, 0),
                     memory_space=smem),
        pl.BlockSpec((ts, d), lambda i: (tile_off + i, 0)),
        pl.BlockSpec((ts, 2), lambda i: (tile_off + i, 0)),
        pl.BlockSpec(memory_space=pl.ANY),
        _const_spec(g.shape),
    ]
    return pl.pallas_call(
        functools.partial(_final_body, n_tiles=n_tiles),
        grid=(n_tiles,),
        in_specs=in_specs,
        out_specs=pl.BlockSpec((ts, d), lambda i: (i, 0)),
        out_shape=jax.ShapeDtypeStruct((n_tiles * ts, d), f32),
        scratch_shapes=[pltpu.VMEM((2, 2 * ts, d // LANES, LANES), y.dtype),
                        pltpu.SemaphoreType.DMA((2,))],
        compiler_params=pltpu.CompilerParams(
            dimension_semantics=("arbitrary",), vmem_limit_bytes=VMEM_LIMIT),
        name="final_norm",
    )(pos_tiles, pos_tiles, x2, gates_t, y, g)


def _router_weights(w_rg, w_re):
    d = w_rg.shape[0]
    wt = jnp.concatenate([w_re.T, w_rg.T, jnp.zeros((ROUTER_ROWS - N_EXPERTS - N_GROUPS, d), f32)], axis=0)
    hi = wt.astype(bf16)
    lo = (wt - hi.astype(f32)).astype(bf16)
    return jnp.concatenate([hi, lo], axis=0), hi


def kernel(x_prompt, x_sample, norm_mix, conf_w_pw1, conf_b_pw1, conf_w_dw, conf_b_dw, conf_ln_g,
           conf_ln_b, conf_w_pw2, conf_b_pw2, sc_w_in, sc_w_dw, sc_w_out, norm_ffn, router_group,
           router_expert, w_gate, w_up, w_down, final_norm):
    bp, sp, d = x_prompt.shape
    bs, ss, _ = x_sample.shape
    ts = TOKEN_TILE
    assert norm_mix.shape[0] == 2 and sp % ts == 0 and ss % ts == 0 and d % LANES == 0
    tp, tsamp = bp * sp, bs * ss
    t = tp + tsamp
    seq_layout = (tp // ts, sp // ts, ss // ts)

    row = lambda a: a.reshape(1, -1)
    wra0, wrb0 = _router_weights(router_group[0], router_expert[0])
    wra1, wrb1 = _router_weights(router_group[1], router_expert[1])
    ti = jnp.arange(ts, dtype=jnp.int32)
    tri = (ti[:, None] < ti[None, :]).astype(bf16)

    x1, h1, eidx1, gates1, lrank1 = _conf_layer(
        x_prompt.reshape(tp, d), x_sample.reshape(tsamp, d), seq_layout, row(norm_mix[0]),
        conf_w_pw1[0].astype(bf16), row(conf_b_pw1[0]),
        conf_w_dw[0], row(conf_b_dw[0]), row(conf_ln_g[0]), row(conf_ln_b[0]),
        conf_w_pw2[0].astype(bf16), row(conf_b_pw2[0]), row(norm_ffn[0]), wra0, wrb0, tri)
    plan1 = _route_plan(eidx1, lrank1, ts)
    y1 = _moe(_dispatch(h1, plan1), plan1, 0, w_gate, w_up, w_down, f32)

    tokw = _window_tokens(t, ts, SC_HALO)
    posw1 = jnp.concatenate([plan1["pos"][0][tokw], plan1["pos"][1][tokw]], axis=1)
    posw1 = posw1.reshape(t // ts, 1, 2 * tokw.shape[1])
    gw1 = gates1.T[tokw]
    x2, h2, eidx2, gates2, lrank2 = _sc_layer(
        x1, y1, posw1, gw1, seq_layout, row(norm_mix[1]), sc_w_in[0].astype(bf16), sc_w_dw[0],
        sc_w_out[0].astype(bf16), row(norm_ffn[1]), wra1, wrb1, tri)
    plan2 = _route_plan(eidx2, lrank2, ts)
    y2 = _moe(_dispatch(h2, plan2), plan2, 1, w_gate, w_up, w_down, bf16)

    fg = row(final_norm)
    gates2_t = gates2.T
    out_p = _final(x2, y2, plan2["pos_tiles"], gates2_t, fg, 0, tp // ts)
    out_s = _final(x2, y2, plan2["pos_tiles"], gates2_t, fg, tp // ts, tsamp // ts)
    return out_p.reshape(bp, sp, d), out_s.reshape(bs, ss, d)
```
